```python
import jax, jax.numpy as jnp
from jax import lax
import numpy as np

D_MODEL = 1024
BATCH = 16
SEQ = 2048
DEPTH = 2
DEC_BATCH = 16
DEC_SEQ = 32
PAST_LEN = 1024

CHUNK = 64
N_A_LAYERS = DEPTH // 2
N_B_LAYERS = DEPTH - N_A_LAYERS
D_RNN = 1408
N_RG_BLOCKS = 16
RG_BLOCK = D_RNN // N_RG_BLOCKS
CONV_W = 4
LRU_C = 8.0
N_HEADS = 16
N_KV_HEADS = 4
HEAD_DIM = 64
GROUP = N_HEADS // N_KV_HEADS
WINDOW = 128
WINDOW_CHUNKS = WINDOW // CHUNK
ROPE_THETA = 10000.0
EPS = 1e-6

kernel_name = "yoco_rglru_swa_sink_stream_step"


def rmsnorm(x, g):
    xf = x.astype(jnp.float32)
    y = xf * lax.rsqrt(jnp.mean(xf * xf, axis=-1, keepdims=True) + EPS)
    return (y * g.astype(jnp.float32)).astype(x.dtype)


def rope(x, pos):
    half = HEAD_DIM // 2
    inv = ROPE_THETA ** (-jnp.arange(half, dtype=jnp.float32) / half)
    ang = pos.astype(jnp.float32)[:, None] * inv[None, :]
    cos = jnp.cos(ang)[None, :, None, :]
    sin = jnp.sin(ang)[None, :, None, :]
    xf = x.astype(jnp.float32)
    x1, x2 = xf[..., :half], xf[..., half:]
    return jnp.concatenate([x1 * cos - x2 * sin, x2 * cos + x1 * sin], axis=-1).astype(x.dtype)


def rglru_block(x, norm_pre, w_in, conv_w, conv_b, w_r, b_r, w_i, b_i, lam, w_out, norm_post,
                conv_buf, h0):
    B, T, _ = x.shape
    u = rmsnorm(x, norm_pre) @ w_in
    branch, gate = u[..., :D_RNN], u[..., D_RNN:]
    ext = jnp.concatenate([conv_buf.astype(branch.dtype), branch], axis=1)
    conv = conv_b + ext[:, 0:T] * conv_w[0]
    for tap in range(1, CONV_W):
        conv = conv + ext[:, tap:tap + T] * conv_w[tap]
    new_conv = ext[:, -(CONV_W - 1):]
    xb = conv.reshape(B, T, N_RG_BLOCKS, RG_BLOCK)
    r = jax.nn.sigmoid(jnp.einsum('btnc,ncd->btnd', xb, w_r).reshape(B, T, D_RNN) + b_r)
    i = jax.nn.sigmoid(jnp.einsum('btnc,ncd->btnd', xb, w_i).reshape(B, T, D_RNN) + b_i)
    log_a = (-LRU_C * r.astype(jnp.float32)) * jax.nn.softplus(-lam.astype(jnp.float32))
    a = jnp.exp(log_a)
    b = jnp.sqrt(-jnp.expm1(2.0 * log_a)) * (i * conv).astype(jnp.float32)
    b = b.at[:, 0].add(a[:, 0] * h0.astype(jnp.float32))

    def combine(lhs, rhs):
        return (lhs[0] * rhs[0], rhs[0] * lhs[1] + rhs[1])

    _, h = lax.associative_scan(combine, (a, b), axis=1)
    new_h = h[:, -1].astype(x.dtype)
    y = (h.astype(x.dtype) * jax.nn.silu(gate)) @ w_out
    return x + rmsnorm(y, norm_post), new_conv, new_h


def shared_kv(h, norm_kv, w_kv, pos):
    B, T, _ = h.shape
    kv = rmsnorm(h, norm_kv) @ w_kv
    kvw = N_KV_HEADS * HEAD_DIM
    k = rope(kv[..., :kvw].reshape(B, T, N_KV_HEADS, HEAD_DIM), pos)
    v = kv[..., kvw:].reshape(B, T, N_KV_HEADS, HEAD_DIM)
    return k, v


def sink_softmax(s, sink):
    m = jnp.maximum(jnp.max(s, axis=-1, keepdims=True), sink)
    p = jnp.exp(s - m)
    return p / (jnp.sum(p, axis=-1, keepdims=True) + jnp.exp(sink - m))


def attend_prompt(q, k, v, sinks):
    B, S = q.shape[:2]
    NC = S // CHUNK
    qb = q.reshape(B, NC, CHUNK, N_KV_HEADS, GROUP, HEAD_DIM)

    def band(t):
        tp = jnp.pad(t, ((0, 0), (WINDOW, 0), (0, 0), (0, 0)))
        tc = tp.reshape(B, NC + WINDOW_CHUNKS, CHUNK, N_KV_HEADS, HEAD_DIM)
        return jnp.concatenate([tc[:, j:j + NC] for j in range(WINDOW_CHUNKS + 1)], axis=2)

    kb, vb = band(k), band(v)
    scale = HEAD_DIM ** -0.5
    s = jnp.einsum('bnqkgd,bnskd->bnkgqs', qb, kb).astype(jnp.float32) * scale
    key_chunk = (jnp.arange(NC)[:, None]
                 + jnp.repeat(jnp.arange(WINDOW_CHUNKS + 1), CHUNK)[None, :] - WINDOW_CHUNKS)
    valid = key_chunk >= 0
    s = jnp.where(valid[None, :, None, None, None, :], s, -jnp.inf)
    sink = sinks.astype(jnp.float32).reshape(N_KV_HEADS, GROUP)[None, None, :, :, None, None]
    p = sink_softmax(s, sink)
    o = jnp.einsum('bnkgqs,bnskd->bnqkgd', p.astype(vb.dtype), vb)
    return o.reshape(B, S, N_HEADS * HEAD_DIM)


def attend_cached(q, k_all, v_all, sinks):
    B, T = q.shape[:2]
    qg = q.reshape(B, T, N_KV_HEADS, GROUP, HEAD_DIM)
    scale = HEAD_DIM ** -0.5
    s = jnp.einsum('btkgd,bskd->bkgts', qg, k_all).astype(jnp.float32) * scale
    sink = sinks.astype(jnp.float32).reshape(N_KV_HEADS, GROUP)[None, :, :, None, None]
    p = sink_softmax(s, sink)
    o = jnp.einsum('bkgts,bskd->btkgd', p.astype(v_all.dtype), v_all)
    return o.reshape(B, T, N_HEADS * HEAD_DIM)


def swa_block(x, pos, norm_pre, w_in, sinks, w_out, norm_post, k, v, attend):
    B, T, _ = x.shape
    qw = N_HEADS * HEAD_DIM
    u = rmsnorm(x, norm_pre) @ w_in
    q = rope(u[..., :qw].reshape(B, T, N_HEADS, HEAD_DIM), pos)
    o = attend(q, k, v, sinks)
    y = (o * jax.nn.silu(u[..., qw:])) @ w_out
    return x + rmsnorm(y, norm_post)


def run_trunk(x, pos, conv_state, rnn_state, attend, a_w, kv_w, b_w):
    new_conv, new_rnn = [], []
    k = v = None
    for layer in range(DEPTH):
        if layer < N_A_LAYERS:
            x, c, h = rglru_block(x, *[w[layer] for w in a_w], conv_state[layer], rnn_state[layer])
            new_conv.append(c)
            new_rnn.append(h)
        else:
            if layer == N_A_LAYERS:
                k, v = shared_kv(x, kv_w[0], kv_w[1], pos)
            j = layer - N_A_LAYERS
            x = swa_block(x, pos, *[w[j] for w in b_w], k, v, attend)
    return x, jnp.stack(new_conv), jnp.stack(new_rnn), k, v


def setup_inputs(seed: int = 0) -> dict:
    key = jax.random.key(seed)
    ks = jax.random.split(key, 32)
    f32 = jnp.float32
    nrm = lambda k, shape, s: jax.random.normal(k, shape, f32) * s
    u = jax.random.uniform(ks[20], (N_A_LAYERS, D_RNN), f32, 0.9, 0.999)
    a0 = u ** (1.0 / LRU_C)
    lru_lambda = jnp.log(a0) - jnp.log1p(-a0)
    return {
        "x_prompt": nrm(ks[0], (BATCH, SEQ, D_MODEL), 1.0),
        "x_sample": nrm(ks[1], (DEC_BATCH, DEC_SEQ, D_MODEL), 1.0),
        "state_conv": nrm(ks[2], (N_A_LAYERS, DEC_BATCH, CONV_W - 1, D_RNN), 1.0),
        "state_rnn": nrm(ks[3], (N_A_LAYERS, DEC_BATCH, D_RNN), 0.5),
        "cache_k": nrm(ks[4], (DEC_BATCH, WINDOW, N_KV_HEADS, HEAD_DIM), 1.0),
        "cache_v": nrm(ks[5], (DEC_BATCH, WINDOW, N_KV_HEADS, HEAD_DIM), 1.0),
        "norm_pre_a": 1.0 + nrm(ks[6], (N_A_LAYERS, D_MODEL), 0.02),
        "w_in_a": nrm(ks[7], (N_A_LAYERS, D_MODEL, 2 * D_RNN), D_MODEL ** -0.5),
        "conv_w_a": nrm(ks[8], (N_A_LAYERS, CONV_W, D_RNN), CONV_W ** -0.5),
        "conv_b_a": nrm(ks[9], (N_A_LAYERS, D_RNN), 0.01),
        "w_gate_r": nrm(ks[10], (N_A_LAYERS, N_RG_BLOCKS, RG_BLOCK, RG_BLOCK), RG_BLOCK ** -0.5),
        "b_gate_r": nrm(ks[11], (N_A_LAYERS, D_RNN), 0.01),
        "w_gate_i": nrm(ks[12], (N_A_LAYERS, N_RG_BLOCKS, RG_BLOCK, RG_BLOCK), RG_BLOCK ** -0.5),
        "b_gate_i": nrm(ks[13], (N_A_LAYERS, D_RNN), 0.01),
        "lru_lambda": lru_lambda,
        "w_out_a": nrm(ks[14], (N_A_LAYERS, D_RNN, D_MODEL), D_RNN ** -0.5),
        "norm_post_a": 1.0 + nrm(ks[15], (N_A_LAYERS, D_MODEL), 0.02),
        "norm_kv": 1.0 + nrm(ks[16], (D_MODEL,), 0.02),
        "w_kv": nrm(ks[17], (D_MODEL, 2 * N_KV_HEADS * HEAD_DIM), D_MODEL ** -0.5),
        "norm_pre_b": 1.0 + nrm(ks[18], (N_B_LAYERS, D_MODEL), 0.02),
        "w_in_b": nrm(ks[19], (N_B_LAYERS, D_MODEL, 2 * N_HEADS * HEAD_DIM), D_MODEL ** -0.5),
        "attn_sinks": nrm(ks[21], (N_B_LAYERS, N_HEADS), 0.5),
        "w_out_b": nrm(ks[22], (N_B_LAYERS, N_HEADS * HEAD_DIM, D_MODEL), (N_HEADS * HEAD_DIM) ** -0.5),
        "norm_post_b": 1.0 + nrm(ks[23], (N_B_LAYERS, D_MODEL), 0.02),
    }


def reference(x_prompt, x_sample, state_conv, state_rnn, cache_k, cache_v,
              norm_pre_a, w_in_a, conv_w_a, conv_b_a, w_gate_r, b_gate_r, w_gate_i, b_gate_i,
              lru_lambda, w_out_a, norm_post_a, norm_kv, w_kv,
              norm_pre_b, w_in_b, attn_sinks, w_out_b, norm_post_b):
    a_w = (norm_pre_a, w_in_a, conv_w_a, conv_b_a, w_gate_r, b_gate_r, w_gate_i, b_gate_i,
           lru_lambda, w_out_a, norm_post_a)
    kv_w = (norm_kv, w_kv)
    b_w = (norm_pre_b, w_in_b, attn_sinks, w_out_b, norm_post_b)

    bp, sp = x_prompt.shape[0], x_prompt.shape[1]
    zero_conv = jnp.zeros((N_A_LAYERS, bp, CONV_W - 1, D_RNN), x_prompt.dtype)
    zero_rnn = jnp.zeros((N_A_LAYERS, bp, D_RNN), x_prompt.dtype)
    pos_p = jnp.arange(sp)
    y_prompt, p_conv, p_rnn, p_k, p_v = run_trunk(
        x_prompt, pos_p, zero_conv, zero_rnn, attend_prompt, a_w, kv_w, b_w)

    pos_s = PAST_LEN + jnp.arange(x_sample.shape[1])

    def attend_sample(q, k, v, sinks):
        k_all = jnp.concatenate([cache_k.astype(k.dtype), k], axis=1)
        v_all = jnp.concatenate([cache_v.astype(v.dtype), v], axis=1)
        return attend_cached(q, k_all, v_all, sinks)

    y_sample, s_conv, s_rnn, s_k, s_v = run_trunk(
        x_sample, pos_s, state_conv, state_rnn, attend_sample, a_w, kv_w, b_w)

    return (y_prompt, y_sample, p_conv, p_rnn, p_k[:, -WINDOW:], p_v[:, -WINDOW:],
            s_conv, s_rnn, s_k, s_v)
```

```python
import functools

import numpy as np
import jax
import jax.numpy as jnp
from jax import lax
from jax.experimental import pallas as pl
from jax.experimental.pallas import tpu as pltpu

D_MODEL = 1024
D_RNN = 1408
N_RG_BLOCKS = 16
RG_BLOCK = D_RNN // N_RG_BLOCKS
CONV_W = 4
LRU_C = 8.0
N_HEADS = 16
N_KV_HEADS = 4
HEAD_DIM = 64
HALF = HEAD_DIM // 2
GROUP = N_HEADS // N_KV_HEADS
WINDOW = 128
CHUNK = 64
PAST_LEN = 1024
ROPE_THETA = 10000.0
EPS = 1e-6

KV_W = N_KV_HEADS * HEAD_DIM
Q_W = N_HEADS * HEAD_DIM

LANES = 128
SUBLANES = 8
MXU_DIM = 256

GATE_K = 512
N_GATE_TILES = -(-D_RNN // MXU_DIM)

KEY_WIN = 256


def _gate_windows():
    wins = []
    for j in range(N_GATE_TILES):
        c0 = j * MXU_DIM
        c1 = min(c0 + MXU_DIM, D_RNN)
        lo = (c0 // RG_BLOCK) * RG_BLOCK
        hi = ((c1 - 1) // RG_BLOCK + 1) * RG_BLOCK
        k0 = min((lo // LANES) * LANES, D_RNN - GATE_K)
        assert k0 <= lo and hi <= k0 + GATE_K
        wins.append((k0, c0, c1))
    return wins


GATE_WINDOWS = _gate_windows()


def _bdot(a, b):
    return jnp.dot(a, b, preferred_element_type=jnp.float32)


def _sigmoid(x):
    return 1.0 / (1.0 + jnp.exp(-x))


def _trunk_kernel(*refs, T, QG, q_groups, n_keys, band, has_state, pos0, k_out_rows):
    it = iter(refs)
    x_ref = next(it)
    if has_state:
        conv0_ref, h0_ref, kh0_ref, vh0_ref = next(it), next(it), next(it), next(it)
    (g_pre_a, w_in_a, conv_w, conv_b, w_ri, b_r, b_i, lam, w_out_a, g_post_a,
     g_kv, w_kv, g_pre_b, w_in_b, sinks, w_out_b, g_post_b, inv_freq) = [next(it) for _ in range(18)]
    y_ref, tail_out, h_out, k_out, v_out = [next(it) for _ in range(5)]
    ext_ref, hcar_ref, kbuf, vbuf, o_ref, cos_tab, sin_tab = [next(it) for _ in range(7)]

    b_idx = pl.program_id(0)
    t_idx = pl.program_id(1)
    f32 = jnp.float32
    bf16 = jnp.bfloat16

    @pl.when(t_idx == 0)
    def _():
        if has_state:
            ext_ref[0:SUBLANES, :] = conv0_ref[...]
            hcar_ref[...] = jnp.broadcast_to(h0_ref[...], hcar_ref.shape)
            kbuf[0:WINDOW, :] = kh0_ref[...].astype(bf16)
            vbuf[0:WINDOW, :] = vh0_ref[...].astype(bf16)
        else:
            ext_ref[0:SUBLANES, :] = jnp.zeros((SUBLANES, D_RNN), f32)
            hcar_ref[...] = jnp.zeros(hcar_ref.shape, f32)
            kbuf[0:WINDOW, :] = jnp.zeros((WINDOW, KV_W), bf16)
            vbuf[0:WINDOW, :] = jnp.zeros((WINDOW, KV_W), bf16)
        if kbuf.shape[0] > WINDOW + T:
            pad = kbuf.shape[0] - WINDOW - T
            kbuf[WINDOW + T:, :] = jnp.zeros((pad, KV_W), bf16)
            vbuf[WINDOW + T:, :] = jnp.zeros((pad, KV_W), bf16)

    if T >= WINDOW:
        @pl.when(t_idx > 0)
        def _():
            ext_ref[0:SUBLANES, :] = ext_ref[T:T + SUBLANES, :]
            kbuf[0:WINDOW, :] = kbuf[T:T + WINDOW, :]
            vbuf[0:WINDOW, :] = vbuf[T:T + WINDOW, :]

    row0 = pl.multiple_of(t_idx * T, T)

    @pl.when(b_idx == 0)
    def _():
        pos = (lax.broadcasted_iota(jnp.int32, (T, LANES), 0) + (pos0 + t_idx * T)).astype(f32)
        ang = pos * inv_freq[...]
        cos_tab[pl.ds(row0, T), :] = jnp.cos(ang)
        sin_tab[pl.ds(row0, T), :] = jnp.sin(ang)

    def rms_scale(v):
        return lax.rsqrt(jnp.sum(v * v, axis=-1, keepdims=True) * (1.0 / D_MODEL) + EPS)

    x = x_ref[...]
    xn = (x * rms_scale(x) * g_pre_a[...]).astype(bf16)
    u = _bdot(xn, w_in_a[...])
    branch = u[:, :D_RNN]
    gate = u[:, D_RNN:]

    ext_ref[SUBLANES:SUBLANES + T, :] = branch
    conv = conv_b[...] + branch * conv_w[CONV_W - 1:CONV_W, :]
    for tap in range(CONV_W - 1):
        r = SUBLANES - (CONV_W - 1) + tap
        conv = conv + ext_ref[r:r + T, :] * conv_w[tap:tap + 1, :]
    tail_out[...] = branch[T - SUBLANES:, :]

    conv_bf = conv.astype(bf16)
    r_parts, i_parts = [], []
    for j, (k0, c0, c1) in enumerate(GATE_WINDOWS):
        rg = _bdot(conv_bf[:, k0:k0 + GATE_K], w_ri[j])
        r_parts.append(rg[:, :c1 - c0])
        i_parts.append(rg[:, MXU_DIM:MXU_DIM + c1 - c0])
    r_gate = _sigmoid(jnp.concatenate(r_parts, axis=1) + b_r[...])
    i_gate = _sigmoid(jnp.concatenate(i_parts, axis=1) + b_i[...])

    z = -lam[...]
    softplus = jnp.maximum(z, 0.0) + jnp.log1p(jnp.exp(-jnp.abs(z)))
    log_a = (-LRU_C * softplus) * r_gate
    a = jnp.exp(log_a)
    bvec = jnp.sqrt(1.0 - a * a) * (i_gate * conv)

    sub = lax.broadcasted_iota(jnp.int32, (T, D_RNN), 0) & (SUBLANES - 1)
    for s in (1, 2, 4):
        a_s = pltpu.roll(a, s, axis=0)
        b_s = pltpu.roll(bvec, s, axis=0)
        m = sub >= s
        bvec = jnp.where(m, a * b_s + bvec, bvec)
        a = jnp.where(m, a * a_s, a)
    hrow = hcar_ref[SUBLANES - 1:SUBLANES, :]
    hs = []
    for n in range(T // SUBLANES):
        blk = a[n * SUBLANES:(n + 1) * SUBLANES, :] * hrow + bvec[n * SUBLANES:(n + 1) * SUBLANES, :]
        hs.append(blk)
        hrow = blk[SUBLANES - 1:SUBLANES, :]
    hcar_ref[...] = hs[-1]
    h_out[...] = hs[-1]
    h_all = jnp.concatenate(hs, axis=0)

    ya = _bdot((h_all * (gate * _sigmoid(gate))).astype(bf16), w_out_a[...])
    x1 = x + ya * rms_scale(ya) * g_post_a[...]

    x1s = x1 * rms_scale(x1)
    kv = _bdot((x1s * g_kv[...]).astype(bf16), w_kv[...])
    ub = _bdot((x1s * g_pre_b[...]).astype(bf16), w_in_b[...])

    cos = cos_tab[pl.ds(row0, T), :]
    sin = sin_tab[pl.ds(row0, T), :]
    ka, kb, v = kv[:, :LANES], kv[:, LANES:2 * LANES], kv[:, 2 * LANES:]
    k_rot = jnp.concatenate([ka * cos - kb * sin, kb * cos + ka * sin], axis=1)
    k_out[...] = k_rot[T - k_out_rows:, :]
    v_out[...] = v[T - k_out_rows:, :]
    kbuf[WINDOW:WINDOW + T, :] = k_rot.astype(bf16)
    vbuf[WINDOW:WINDOW + T, :] = v.astype(bf16)

    scale = HEAD_DIM ** -0.5
    qcat = []
    for j in range(GROUP):
        qa = ub[:, j * LANES:(j + 1) * LANES]
        qb = ub[:, Q_W // 2 + j * LANES:Q_W // 2 + (j + 1) * LANES]
        qcat.append((jnp.concatenate([qa * cos - qb * sin, qb * cos + qa * sin], axis=1) * scale).astype(bf16))
    gate2 = ub[:, Q_W:]

    lane_q = lax.broadcasted_iota(jnp.int32, (QG, 2 * LANES), 1)
    q_slot = (lane_q & (LANES - 1)) // HALF
    o_slot = lane_q // HEAD_DIM
    row_k = lax.broadcasted_iota(jnp.int32, (QG, KEY_WIN), 0)
    col_k = lax.broadcasted_iota(jnp.int32, (QG, KEY_WIN), 1)

    for gi, (q0, k0) in enumerate(q_groups):
        invalid = col_k >= n_keys
        if band:
            invalid = invalid | ((row_k < CHUNK) & (col_k >= KEY_WIN - CHUNK))
            invalid = invalid | ((row_k >= CHUNK) & (col_k < CHUNK))
        if not has_state and k0 < WINDOW:
            invalid = invalid | (col_k < jnp.where(t_idx == 0, WINDOW - k0, 0))
        kwin = kbuf[k0:k0 + KEY_WIN, :]
        vwin = vbuf[k0:k0 + KEY_WIN, :]
        zero = jnp.zeros((QG, 2 * LANES), bf16)
        qstack = jnp.concatenate(
            [jnp.where(q_slot == h // GROUP, qcat[h % GROUP][q0:q0 + QG, :], zero) for h in range(N_HEADS)],
            axis=0)
        s_all = lax.dot_general(qstack, kwin, (((1,), (1,)), ((), ())),
                                preferred_element_type=f32)
        p_parts, inv_l = [], []
        for h in range(N_HEADS):
            sink = sinks[h]
            sh = jnp.where(invalid, -jnp.inf, s_all[h * QG:(h + 1) * QG, :])
            mh = jnp.maximum(jnp.max(sh, axis=-1, keepdims=True), sink)
            ph = jnp.exp(sh - mh)
            lh = jnp.sum(ph, axis=-1, keepdims=True) + jnp.exp(sink - mh)
            p_parts.append(ph.astype(bf16))
            inv_l.append(1.0 / lh)
        o_all = _bdot(jnp.concatenate(p_parts, axis=0), vwin)
        for j in range(GROUP):
            acc = jnp.zeros((QG, 2 * LANES), f32)
            for g in range(N_KV_HEADS):
                h = g * GROUP + j
                acc = jnp.where(o_slot == g, o_all[h * QG:(h + 1) * QG, :] * inv_l[h], acc)
            o_ref[q0:q0 + QG, j * 2 * LANES:(j + 1) * 2 * LANES] = acc

    og = (o_ref[...] * (gate2 * _sigmoid(gate2))).astype(bf16)
    yb = _bdot(og, w_out_b[...])
    y_ref[...] = x1 + yb * rms_scale(yb) * g_post_b[...]


def _const_spec(shape):
    nd = len(shape)
    return pl.BlockSpec(shape, lambda b, t, _nd=nd: (0,) * _nd, pipeline_mode=pl.Buffered(1))


def _run_trunk(x, state, weights, *, T, QG, q_groups, n_keys, band, pos0, k_out_rows, name):
    B, S, _ = x.shape
    NT = S // T
    assert NT * T == S and (NT == 1 or T >= WINDOW)
    has_state = state is not None
    kb_rows = max(WINDOW + T, KEY_WIN)

    in_specs = [pl.BlockSpec((None, T, D_MODEL), lambda b, t: (b, t, 0))]
    args = [x]
    if has_state:
        conv0, h0, kh0, vh0 = state
        in_specs += [
            pl.BlockSpec((None, SUBLANES, D_RNN), lambda b, t: (b, 0, 0)),
            pl.BlockSpec((None, 1, D_RNN), lambda b, t: (b, 0, 0)),
            pl.BlockSpec((None, WINDOW, KV_W), lambda b, t: (b, 0, 0)),
            pl.BlockSpec((None, WINDOW, KV_W), lambda b, t: (b, 0, 0)),
        ]
        args += [conv0, h0, kh0, vh0]
    for name_w, w in weights:
        if name_w == "sinks":
            in_specs.append(pl.BlockSpec(memory_space=pltpu.SMEM))
        else:
            in_specs.append(_const_spec(w.shape))
        args.append(w)

    out_shape = (
        jax.ShapeDtypeStruct((B, S, D_MODEL), jnp.float32),
        jax.ShapeDtypeStruct((B, SUBLANES, D_RNN), jnp.float32),
        jax.ShapeDtypeStruct((B, SUBLANES, D_RNN), jnp.float32),
        jax.ShapeDtypeStruct((B, k_out_rows, KV_W), jnp.float32),
        jax.ShapeDtypeStruct((B, k_out_rows, KV_W), jnp.float32),
    )
    out_specs = (
        pl.BlockSpec((None, T, D_MODEL), lambda b, t: (b, t, 0)),
        pl.BlockSpec((None, SUBLANES, D_RNN), lambda b, t: (b, 0, 0)),
        pl.BlockSpec((None, SUBLANES, D_RNN), lambda b, t: (b, 0, 0)),
        pl.BlockSpec((None, k_out_rows, KV_W), lambda b, t: (b, 0, 0)),
        pl.BlockSpec((None, k_out_rows, KV_W), lambda b, t: (b, 0, 0)),
    )
    scratch = [
        pltpu.VMEM((T + SUBLANES, D_RNN), jnp.float32),
        pltpu.VMEM((SUBLANES, D_RNN), jnp.float32),
        pltpu.VMEM((kb_rows, KV_W), jnp.bfloat16),
        pltpu.VMEM((kb_rows, KV_W), jnp.bfloat16),
        pltpu.VMEM((T, Q_W), jnp.float32),
        pltpu.VMEM((S, LANES), jnp.float32),
        pltpu.VMEM((S, LANES), jnp.float32),
    ]
    kern = functools.partial(_trunk_kernel, T=T, QG=QG, q_groups=q_groups, n_keys=n_keys, band=band,
                             has_state=has_state, pos0=pos0, k_out_rows=k_out_rows)
    return pl.pallas_call(
        kern,
        grid=(B, NT),
        in_specs=in_specs,
        out_specs=out_specs,
        out_shape=out_shape,
        scratch_shapes=scratch,
        compiler_params=pltpu.CompilerParams(
            dimension_semantics=("arbitrary", "arbitrary"),
            vmem_limit_bytes=56 * 1024 * 1024),
        name=name,
    )(*args)


def _prep_weights(norm_pre_a, w_in_a, conv_w_a, conv_b_a, w_gate_r, b_gate_r, w_gate_i, b_gate_i,
                  lru_lambda, w_out_a, norm_post_a, norm_kv, w_kv, norm_pre_b, w_in_b, attn_sinks,
                  w_out_b, norm_post_b):
    bf16 = jnp.bfloat16
    row = lambda p: p.reshape(1, -1).astype(jnp.float32)

    def dense_block_diag(w):
        eye = jnp.eye(N_RG_BLOCKS, dtype=w.dtype)
        return (w[:, :, None, :] * eye[:, None, :, None]).reshape(D_RNN, D_RNN)

    wr, wi = dense_block_diag(w_gate_r[0]), dense_block_diag(w_gate_i[0])
    tiles = []
    for k0, c0, c1 in GATE_WINDOWS:
        pad = MXU_DIM - (c1 - c0)
        tr = jnp.pad(wr[k0:k0 + GATE_K, c0:c1], ((0, 0), (0, pad)))
        ti = jnp.pad(wi[k0:k0 + GATE_K, c0:c1], ((0, 0), (0, pad)))
        tiles.append(jnp.concatenate([tr, ti], axis=1))
    w_ri = jnp.stack(tiles).astype(bf16)

    d = np.arange(HALF)
    k_a = (np.arange(N_KV_HEADS)[:, None] * HEAD_DIM + d[None, :]).reshape(-1)
    kv_cols = np.concatenate([k_a, k_a + HALF, KV_W + np.arange(KV_W)])
    j_, g_ = np.meshgrid(np.arange(GROUP), np.arange(N_KV_HEADS), indexing="ij")
    q_a = (((g_ * GROUP + j_) * HEAD_DIM)[:, :, None] + d[None, None, :]).reshape(-1)
    o_cols = (((g_ * GROUP + j_) * HEAD_DIM)[:, :, None] + np.arange(HEAD_DIM)[None, None, :]).reshape(-1)
    inb_cols = np.concatenate([q_a, q_a + HALF, Q_W + o_cols])

    inv = ROPE_THETA ** (-jnp.arange(HALF, dtype=jnp.float32) / HALF)
    inv_freq = jnp.tile(inv, LANES // HALF).reshape(1, LANES)

    return [
        ("g_pre_a", row(norm_pre_a[0])),
        ("w_in_a", w_in_a[0].astype(bf16)),
        ("conv_w", conv_w_a[0].astype(jnp.float32)),
        ("conv_b", row(conv_b_a[0])),
        ("w_ri", w_ri),
        ("b_r", row(b_gate_r[0])),
        ("b_i", row(b_gate_i[0])),
        ("lam", row(lru_lambda[0])),
        ("w_out_a", w_out_a[0].astype(bf16)),
        ("g_post_a", row(norm_post_a[0])),
        ("g_kv", row(norm_kv)),
        ("w_kv", w_kv[:, kv_cols].astype(bf16)),
        ("g_pre_b", row(norm_pre_b[0])),
        ("w_in_b", w_in_b[0][:, inb_cols].astype(bf16)),
        ("sinks", attn_sinks[0].astype(jnp.float32)),
        ("w_out_b", w_out_b[0][o_cols, :].astype(bf16)),
        ("g_post_b", row(norm_post_b[0])),
        ("inv_freq", inv_freq),
    ]


def _k_to_cache_layout(k):
    B, R, _ = k.shape
    ka = k[..., :LANES].reshape(B, R, N_KV_HEADS, HALF)
    kb = k[..., LANES:].reshape(B, R, N_KV_HEADS, HALF)
    return jnp.concatenate([ka, kb], axis=-1)


def _k_from_cache_layout(k):
    B, R = k.shape[:2]
    return jnp.concatenate([k[..., :HALF].reshape(B, R, LANES), k[..., HALF:].reshape(B, R, LANES)], axis=-1)


def kernel(x_prompt, x_sample, state_conv, state_rnn, cache_k, cache_v, norm_pre_a, w_in_a, conv_w_a, conv_b_a, w_gate_r, b_gate_r, w_gate_i, b_gate_i, lru_lambda, w_out_a, norm_post_a, norm_kv, w_kv, norm_pre_b, w_in_b, attn_sinks, w_out_b, norm_post_b):
    weights = _prep_weights(norm_pre_a, w_in_a, conv_w_a, conv_b_a, w_gate_r, b_gate_r, w_gate_i, b_gate_i,
                            lru_lambda, w_out_a, norm_post_a, norm_kv, w_kv, norm_pre_b, w_in_b,
                            attn_sinks, w_out_b, norm_post_b)
    tail = CONV_W - 1

    T_P = 256
    yp, p_tail, p_h, p_k, p_v = _run_trunk(
        x_prompt, None, weights, T=T_P, QG=2 * CHUNK,
        q_groups=tuple((q0, q0) for q0 in range(0, T_P, 2 * CHUNK)),
        n_keys=KEY_WIN, band=True, pos0=0, k_out_rows=WINDOW, name="trunk_prompt")

    Bs, Ts, _ = x_sample.shape
    conv0 = jnp.pad(state_conv[0], ((0, 0), (SUBLANES - tail, 0), (0, 0)))
    h0 = state_rnn[0][:, None, :]
    kh0 = _k_from_cache_layout(cache_k)
    vh0 = cache_v.reshape(Bs, WINDOW, KV_W)
    ys, s_tail, s_h, s_k, s_v = _run_trunk(
        x_sample, (conv0, h0, kh0, vh0), weights, T=Ts, QG=Ts, q_groups=((0, 0),),
        n_keys=WINDOW + Ts, band=False, pos0=PAST_LEN, k_out_rows=Ts, name="trunk_sample")

    def unpack(t_out, h_out, k_o, v_o):
        B, R = k_o.shape[:2]
        return (t_out[:, SUBLANES - tail:, :][None], h_out[:, SUBLANES - 1, :][None],
                _k_to_cache_layout(k_o), v_o.reshape(B, R, N_KV_HEADS, HEAD_DIM))

    pc, pr, pk, pv = unpack(p_tail, p_h, p_k, p_v)
    sc, sr, sk, sv = unpack(s_tail, s_h, s_k, s_v)
    return (yp, ys, pc, pr, pk, pv, sc, sr, sk, sv)
```

```python
import functools

import numpy as np
import jax
import jax.numpy as jnp
from jax import lax
from jax.experimental import pallas as pl
from jax.experimental.pallas import tpu as pltpu

D_MODEL = 1024
D_RNN = 1408
N_RG_BLOCKS = 16
RG_BLOCK = D_RNN // N_RG_BLOCKS
CONV_W = 4
LRU_C = 8.0
N_HEADS = 16
N_KV_HEADS = 4
HEAD_DIM = 64
HALF = HEAD_DIM // 2
GROUP = N_HEADS // N_KV_HEADS
WINDOW = 128
CHUNK = 64
PAST_LEN = 1024
ROPE_THETA = 10000.0
EPS = 1e-6

KV_W = N_KV_HEADS * HEAD_DIM
Q_W = N_HEADS * HEAD_DIM

LANES = 128
SUBLANES = 8
MXU_DIM = 256

NB = SUBLANES
TAIL = (CONV_W - 1) * NB

GATE_K = 512
N_GATE_TILES = -(-D_RNN // MXU_DIM)

KEY_WIN = 256


def _gate_windows():
    wins = []
    for j in range(N_GATE_TILES):
        c0 = j * MXU_DIM
        c1 = min(c0 + MXU_DIM, D_RNN)
        lo = (c0 // RG_BLOCK) * RG_BLOCK
        hi = ((c1 - 1) // RG_BLOCK + 1) * RG_BLOCK
        k0 = min((lo // LANES) * LANES, D_RNN - GATE_K)
        assert k0 <= lo and hi <= k0 + GATE_K
        wins.append((k0, c0, c1))
    return wins


GATE_WINDOWS = _gate_windows()


def _bdot(a, b):
    return jnp.dot(a, b, preferred_element_type=jnp.float32)


def _sigmoid(x):
    return 1.0 / (1.0 + jnp.exp(-x))


def _trunk_kernel(*refs, TT, NT, NG, has_state, pos0, n_kout):
    it = iter(refs)
    x_hbm = next(it)
    if has_state:
        conv0_ref, h0_ref, kh0_ref, vh0_ref = next(it), next(it), next(it), next(it)
    (g_pre_a, w_in_a, conv_w, conv_b, w_ri, b_r, b_i, lam, w_out_a, g_post_a,
     g_kv, w_kv, g_pre_b, w_in_b, sinks, w_out_b, g_post_b, inv_freq) = [next(it) for _ in range(18)]
    y_hbm, tail_out, h_out, k_out, v_out = [next(it) for _ in range(5)]
    (in_buf, out_buf, sem_in, sem_out, ext_ref, tail_ref, hcar_ref, kbuf, vbuf,
     q_il, k_il, v_il, o_il, rope_small, cos_tab, sin_tab) = [next(it) for _ in range(16)]

    g_idx = pl.program_id(0)
    t_idx = pl.program_id(1)
    step = g_idx * NT + t_idx
    n_steps = NT * NG
    slot = step & 1
    R = TT * NB
    f32 = jnp.float32
    bf16 = jnp.bfloat16

    def in_copies(t, g, s):
        return [pltpu.make_async_copy(x_hbm.at[g * NB + b, pl.ds(t * TT, TT), :],
                                      in_buf.at[s, :, b, :], sem_in.at[s]) for b in range(NB)]

    def out_copies(t, g, s):
        return [pltpu.make_async_copy(out_buf.at[s, :, b, :],
                                      y_hbm.at[g * NB + b, pl.ds(t * TT, TT), :], sem_out.at[s]) for b in range(NB)]

    @pl.when(step == 0)
    def _():
        for c in in_copies(t_idx, g_idx, slot):
            c.start()

    @pl.when(step + 1 < n_steps)
    def _():
        nxt = step + 1
        for c in in_copies(nxt % NT, nxt // NT, 1 - slot):
            c.start()

    for c in in_copies(t_idx, g_idx, slot):
        c.wait()

    if has_state:
        @pl.when(t_idx == 0)
        def _():
            ext_ref[0:TAIL, :] = conv0_ref[...]
            hcar_ref[...] = h0_ref[...]
            zpad = jnp.zeros((NB, KEY_WIN - WINDOW, KV_W), bf16)
            kbuf[:, 0:WINDOW, :] = kh0_ref[...].astype(bf16)
            vbuf[:, 0:WINDOW, :] = vh0_ref[...].astype(bf16)
            kbuf[:, WINDOW:, :] = zpad
            vbuf[:, WINDOW:, :] = zpad
    else:
        @pl.when(t_idx == 0)
        def _():
            ext_ref[0:TAIL, :] = jnp.zeros((TAIL, D_RNN), f32)
            hcar_ref[...] = jnp.zeros((NB, D_RNN), f32)
            kbuf[...] = jnp.zeros((NB, KEY_WIN, KV_W), bf16)
            vbuf[...] = jnp.zeros((NB, KEY_WIN, KV_W), bf16)

    @pl.when(t_idx > 0)
    def _():
        ext_ref[0:TAIL, :] = tail_ref[...]

    f0 = pl.multiple_of(t_idx * TT, TT)

    @pl.when(g_idx == 0)
    def _():
        pos = (lax.broadcasted_iota(jnp.int32, (TT, LANES), 0) + (pos0 + t_idx * TT)).astype(f32)
        ang = pos * inv_freq[...]
        rope_small[0, pl.ds(f0, TT), :] = jnp.cos(ang)
        rope_small[1, pl.ds(f0, TT), :] = jnp.sin(ang)

    for t in range(TT):
        cos_tab[t * NB:(t + 1) * NB, :] = jnp.broadcast_to(rope_small[0, pl.ds(f0 + t, 1), :], (NB, LANES))
        sin_tab[t * NB:(t + 1) * NB, :] = jnp.broadcast_to(rope_small[1, pl.ds(f0 + t, 1), :], (NB, LANES))

    def rms_scale(v):
        return lax.rsqrt(jnp.sum(v * v, axis=-1, keepdims=True) * (1.0 / D_MODEL) + EPS)

    x = in_buf[slot].reshape(R, D_MODEL)
    xn = (x * rms_scale(x) * g_pre_a[...]).astype(bf16)
    u = _bdot(xn, w_in_a[...])
    branch = u[:, :D_RNN]
    gate = u[:, D_RNN:]

    ext_ref[TAIL:TAIL + R, :] = branch
    conv = conv_b[...] + branch * conv_w[CONV_W - 1:CONV_W, :]
    for tap in range(CONV_W - 1):
        conv = conv + ext_ref[tap * NB:tap * NB + R, :] * conv_w[tap:tap + 1, :]
    new_tail = branch[R - TAIL:, :]
    tail_ref[...] = new_tail
    tail_out[...] = new_tail

    conv_bf = conv.astype(bf16)
    r_parts, i_parts = [], []
    for j, (k0, c0, c1) in enumerate(GATE_WINDOWS):
        rg = _bdot(conv_bf[:, k0:k0 + GATE_K], w_ri[j])
        r_parts.append(rg[:, :c1 - c0])
        i_parts.append(rg[:, MXU_DIM:MXU_DIM + c1 - c0])
    r_gate = _sigmoid(jnp.concatenate(r_parts, axis=1) + b_r[...])
    i_gate = _sigmoid(jnp.concatenate(i_parts, axis=1) + b_i[...])

    z = -lam[...]
    softplus = jnp.maximum(z, 0.0) + jnp.log1p(jnp.exp(-jnp.abs(z)))
    log_a = (-LRU_C * softplus) * r_gate
    a = jnp.exp(log_a)
    bvec = jnp.sqrt(1.0 - a * a) * (i_gate * conv)

    h = hcar_ref[...]
    hs = []
    for t in range(TT):
        h = a[t * NB:(t + 1) * NB, :] * h + bvec[t * NB:(t + 1) * NB, :]
        hs.append(h)
    hcar_ref[...] = h
    h_out[...] = h
    h_all = jnp.concatenate(hs, axis=0)

    ya = _bdot((h_all * (gate * _sigmoid(gate))).astype(bf16), w_out_a[...])
    x1 = x + ya * rms_scale(ya) * g_post_a[...]

    x1s = x1 * rms_scale(x1)
    kv = _bdot((x1s * g_kv[...]).astype(bf16), w_kv[...])
    ub = _bdot((x1s * g_pre_b[...]).astype(bf16), w_in_b[...])

    cos = cos_tab[...]
    sin = sin_tab[...]
    ka, kb, v = kv[:, :LANES], kv[:, LANES:2 * LANES], kv[:, 2 * LANES:]
    k_il[0] = ka * cos - kb * sin
    k_il[1] = kb * cos + ka * sin
    v_il[0] = v[:, :LANES]
    v_il[1] = v[:, LANES:]
    k_out[...] = jnp.concatenate([k_il[0], k_il[1]], axis=1)
    v_out[...] = v

    scale = HEAD_DIM ** -0.5
    for j in range(GROUP):
        qa = ub[:, j * LANES:(j + 1) * LANES]
        qb = ub[:, Q_W // 2 + j * LANES:Q_W // 2 + (j + 1) * LANES]
        q_il[j] = (qa * cos - qb * sin) * scale
        q_il[GROUP + j] = (qb * cos + qa * sin) * scale
    gate2 = ub[:, Q_W:]

    if has_state:
        cur_off = WINDOW
        n_valid = WINDOW + TT
    else:
        ring = WINDOW // TT + 1
        cur_off = pl.multiple_of((t_idx % ring) * TT, TT)
        n_valid = jnp.minimum(t_idx + 1, ring) * TT

    lane_q = lax.broadcasted_iota(jnp.int32, (TT, 2 * LANES), 1)
    q_slot = (lane_q & (LANES - 1)) // HALF
    o_slot = lane_q // HEAD_DIM
    invalid = lax.broadcasted_iota(jnp.int32, (TT, KEY_WIN), 1) >= n_valid
    zero_q = jnp.zeros((TT, 2 * LANES), bf16)

    def attend_stream(b, carry):
        rows = pl.ds(b, TT, stride=NB)
        kbuf[b, pl.ds(cur_off, TT), :] = jnp.concatenate(
            [k_il[0, rows, :], k_il[1, rows, :]], axis=1).astype(bf16)
        vbuf[b, pl.ds(cur_off, TT), :] = jnp.concatenate(
            [v_il[0, rows, :], v_il[1, rows, :]], axis=1).astype(bf16)
        qcat = [jnp.concatenate([q_il[j, rows, :], q_il[GROUP + j, rows, :]], axis=1).astype(bf16)
                for j in range(GROUP)]
        qstack = jnp.concatenate(
            [jnp.where(q_slot == h // GROUP, qcat[h % GROUP], zero_q) for h in range(N_HEADS)], axis=0)
        s_all = lax.dot_general(qstack, kbuf[b], (((1,), (1,)), ((), ())),
                                preferred_element_type=f32)
        p_parts, inv_l = [], []
        for h in range(N_HEADS):
            sink = sinks[h]
            sh = jnp.where(invalid, -jnp.inf, s_all[h * TT:(h + 1) * TT, :])
            mh = jnp.maximum(jnp.max(sh, axis=-1, keepdims=True), sink)
            ph = jnp.exp(sh - mh)
            lh = jnp.sum(ph, axis=-1, keepdims=True) + jnp.exp(sink - mh)
            p_parts.append(ph.astype(bf16))
            inv_l.append(1.0 / lh)
        o_all = _bdot(jnp.concatenate(p_parts, axis=0), vbuf[b])
        for j in range(GROUP):
            acc = jnp.zeros((TT, 2 * LANES), f32)
            for g in range(N_KV_HEADS):
                h = g * GROUP + j
                acc = jnp.where(o_slot == g, o_all[h * TT:(h + 1) * TT, :] * inv_l[h], acc)
            o_il[2 * j, rows, :] = acc[:, :LANES]
            o_il[2 * j + 1, rows, :] = acc[:, LANES:]
        return carry

    lax.fori_loop(0, NB, attend_stream, 0)

    o2 = jnp.concatenate([o_il[l] for l in range(Q_W // LANES)], axis=1)
    og = (o2 * (gate2 * _sigmoid(gate2))).astype(bf16)
    yb = _bdot(og, w_out_b[...])
    y = x1 + yb * rms_scale(yb) * g_post_b[...]

    @pl.when(step >= 2)
    def _():
        for c in out_copies(t_idx, g_idx, slot):
            c.wait()

    out_buf[slot] = y.reshape(TT, NB, D_MODEL)
    for c in out_copies(t_idx, g_idx, slot):
        c.start()

    @pl.when(step == n_steps - 1)
    def _():
        if n_steps >= 2:
            for c in out_copies(t_idx, g_idx, 1 - slot):
                c.wait()
        for c in out_copies(t_idx, g_idx, slot):
            c.wait()


def _const_spec(shape):
    nd = len(shape)
    return pl.BlockSpec(shape, lambda g, t, _nd=nd: (0,) * _nd, pipeline_mode=pl.Buffered(1))


def _run_trunk(x, state, weights, *, TT, pos0, n_kout, name):
    B, S, _ = x.shape
    NT = S // TT
    NG = B // NB
    R = TT * NB
    assert NT * TT == S and NG * NB == B and n_kout <= NT
    assert state is not None or WINDOW % TT == 0
    has_state = state is not None

    in_specs = [pl.BlockSpec(memory_space=pl.ANY)]
    args = [x]
    if has_state:
        conv0, h0, kh0, vh0 = state
        in_specs += [
            pl.BlockSpec((None, TAIL, D_RNN), lambda g, t: (g, 0, 0)),
            pl.BlockSpec((None, NB, D_RNN), lambda g, t: (g, 0, 0)),
            pl.BlockSpec((NB, WINDOW, KV_W), lambda g, t: (g, 0, 0)),
            pl.BlockSpec((NB, WINDOW, KV_W), lambda g, t: (g, 0, 0)),
        ]
        args += [conv0, h0, kh0, vh0]
    for name_w, w in weights:
        if name_w == "sinks":
            in_specs.append(pl.BlockSpec(memory_space=pltpu.SMEM))
        else:
            in_specs.append(_const_spec(w.shape))
        args.append(w)

    kout_map = lambda g, t: (g, jnp.maximum(t - (NT - n_kout), 0), 0, 0)
    out_shape = (
        jax.ShapeDtypeStruct((B, S, D_MODEL), jnp.float32),
        jax.ShapeDtypeStruct((NG, TAIL, D_RNN), jnp.float32),
        jax.ShapeDtypeStruct((NG, NB, D_RNN), jnp.float32),
        jax.ShapeDtypeStruct((NG, n_kout, R, KV_W), jnp.float32),
        jax.ShapeDtypeStruct((NG, n_kout, R, KV_W), jnp.float32),
    )
    out_specs = (
        pl.BlockSpec(memory_space=pl.ANY),
        pl.BlockSpec((None, TAIL, D_RNN), lambda g, t: (g, 0, 0)),
        pl.BlockSpec((None, NB, D_RNN), lambda g, t: (g, 0, 0)),
        pl.BlockSpec((None, None, R, KV_W), kout_map),
        pl.BlockSpec((None, None, R, KV_W), kout_map),
    )
    scratch = [
        pltpu.VMEM((2, TT, NB, D_MODEL), jnp.float32),
        pltpu.VMEM((2, TT, NB, D_MODEL), jnp.float32),
        pltpu.SemaphoreType.DMA((2,)),
        pltpu.SemaphoreType.DMA((2,)),
        pltpu.VMEM((TAIL + R, D_RNN), jnp.float32),
        pltpu.VMEM((TAIL, D_RNN), jnp.float32),
        pltpu.VMEM((NB, D_RNN), jnp.float32),
        pltpu.VMEM((NB, KEY_WIN, KV_W), jnp.bfloat16),
        pltpu.VMEM((NB, KEY_WIN, KV_W), jnp.bfloat16),
        pltpu.VMEM((Q_W // LANES, R, LANES), jnp.float32),
        pltpu.VMEM((KV_W // LANES, R, LANES), jnp.float32),
        pltpu.VMEM((KV_W // LANES, R, LANES), jnp.float32),
        pltpu.VMEM((Q_W // LANES, R, LANES), jnp.float32),
        pltpu.VMEM((2, S, LANES), jnp.float32),
        pltpu.VMEM((R, LANES), jnp.float32),
        pltpu.VMEM((R, LANES), jnp.float32),
    ]
    kern = functools.partial(_trunk_kernel, TT=TT, NT=NT, NG=NG, has_state=has_state, pos0=pos0, n_kout=n_kout)
    return pl.pallas_call(
        kern,
        grid=(NG, NT),
        in_specs=in_specs,
        out_specs=out_specs,
        out_shape=out_shape,
        scratch_shapes=scratch,
        compiler_params=pltpu.CompilerParams(
            dimension_semantics=("arbitrary", "arbitrary"),
            vmem_limit_bytes=58 * 1024 * 1024),
        name=name,
    )(*args)


def _prep_weights(norm_pre_a, w_in_a, conv_w_a, conv_b_a, w_gate_r, b_gate_r, w_gate_i, b_gate_i,
                  lru_lambda, w_out_a, norm_post_a, norm_kv, w_kv, norm_pre_b, w_in_b, attn_sinks,
                  w_out_b, norm_post_b):
    bf16 = jnp.bfloat16
    row = lambda p: p.reshape(1, -1).astype(jnp.float32)

    def dense_block_diag(w):
        eye = jnp.eye(N_RG_BLOCKS, dtype=w.dtype)
        return (w[:, :, None, :] * eye[:, None, :, None]).reshape(D_RNN, D_RNN)

    wr, wi = dense_block_diag(w_gate_r[0]), dense_block_diag(w_gate_i[0])
    tiles = []
    for k0, c0, c1 in GATE_WINDOWS:
        pad = MXU_DIM - (c1 - c0)
        tr = jnp.pad(wr[k0:k0 + GATE_K, c0:c1], ((0, 0), (0, pad)))
        ti = jnp.pad(wi[k0:k0 + GATE_K, c0:c1], ((0, 0), (0, pad)))
        tiles.append(jnp.concatenate([tr, ti], axis=1))
    w_ri = jnp.stack(tiles).astype(bf16)

    d = np.arange(HALF)
    k_a = (np.arange(N_KV_HEADS)[:, None] * HEAD_DIM + d[None, :]).reshape(-1)
    kv_cols = np.concatenate([k_a, k_a + HALF, KV_W + np.arange(KV_W)])
    j_, g_ = np.meshgrid(np.arange(GROUP), np.arange(N_KV_HEADS), indexing="ij")
    q_a = (((g_ * GROUP + j_) * HEAD_DIM)[:, :, None] + d[None, None, :]).reshape(-1)
    o_cols = (((g_ * GROUP + j_) * HEAD_DIM)[:, :, None] + np.arange(HEAD_DIM)[None, None, :]).reshape(-1)
    inb_cols = np.concatenate([q_a, q_a + HALF, Q_W + o_cols])

    inv = ROPE_THETA ** (-jnp.arange(HALF, dtype=jnp.float32) / HALF)
    inv_freq = jnp.tile(inv, LANES // HALF).reshape(1, LANES)

    return [
        ("g_pre_a", row(norm_pre_a[0])),
        ("w_in_a", w_in_a[0].astype(bf16)),
        ("conv_w", conv_w_a[0].astype(jnp.float32)),
        ("conv_b", row(conv_b_a[0])),
        ("w_ri", w_ri),
        ("b_r", row(b_gate_r[0])),
        ("b_i", row(b_gate_i[0])),
        ("lam", row(lru_lambda[0])),
        ("w_out_a", w_out_a[0].astype(bf16)),
        ("g_post_a", row(norm_post_a[0])),
        ("g_kv", row(norm_kv)),
        ("w_kv", w_kv[:, kv_cols].astype(bf16)),
        ("g_pre_b", row(norm_pre_b[0])),
        ("w_in_b", w_in_b[0][:, inb_cols].astype(bf16)),
        ("sinks", attn_sinks[0].astype(jnp.float32)),
        ("w_out_b", w_out_b[0][o_cols, :].astype(bf16)),
        ("g_post_b", row(norm_post_b[0])),
        ("inv_freq", inv_freq),
    ]


def _k_to_cache_layout(k):
    B, R, _ = k.shape
    ka = k[..., :LANES].reshape(B, R, N_KV_HEADS, HALF)
    kb = k[..., LANES:].reshape(B, R, N_KV_HEADS, HALF)
    return jnp.concatenate([ka, kb], axis=-1)


def _k_from_cache_layout(k):
    B, R = k.shape[:2]
    return jnp.concatenate([k[..., :HALF].reshape(B, R, LANES), k[..., HALF:].reshape(B, R, LANES)], axis=-1)


def _to_streams(a, frames):
    NG, n, _, C = a.shape
    return a.reshape(NG, n, frames, NB, C).transpose(0, 3, 1, 2, 4).reshape(NG * NB, n * frames, C)


def kernel(x_prompt, x_sample, state_conv, state_rnn, cache_k, cache_v, norm_pre_a, w_in_a, conv_w_a, conv_b_a, w_gate_r, b_gate_r, w_gate_i, b_gate_i, lru_lambda, w_out_a, norm_post_a, norm_kv, w_kv, norm_pre_b, w_in_b, attn_sinks, w_out_b, norm_post_b):
    weights = _prep_weights(norm_pre_a, w_in_a, conv_w_a, conv_b_a, w_gate_r, b_gate_r, w_gate_i, b_gate_i,
                            lru_lambda, w_out_a, norm_post_a, norm_kv, w_kv, norm_pre_b, w_in_b,
                            attn_sinks, w_out_b, norm_post_b)
    tail = CONV_W - 1

    def unpack(y, t_out, h_out, k_o, v_o, frames):
        NG = t_out.shape[0]
        conv = t_out.reshape(NG, tail, NB, D_RNN).transpose(0, 2, 1, 3).reshape(NG * NB, tail, D_RNN)
        k = _k_to_cache_layout(_to_streams(k_o, frames))
        v = _to_streams(v_o, frames)
        return (y, conv[None], h_out.reshape(NG * NB, D_RNN)[None], k,
                v.reshape(v.shape[0], v.shape[1], N_KV_HEADS, HEAD_DIM))

    prompt = _run_trunk(x_prompt, None, weights, TT=CHUNK, pos0=0, n_kout=WINDOW // CHUNK, name="trunk_prompt")
    yp, pc, pr, pk, pv = unpack(*prompt, CHUNK)

    Bs, Ts, _ = x_sample.shape
    NGs = Bs // NB
    conv0 = state_conv[0].reshape(NGs, NB, tail, D_RNN).transpose(0, 2, 1, 3).reshape(NGs, TAIL, D_RNN)
    h0 = state_rnn[0].reshape(NGs, NB, D_RNN)
    kh0 = _k_from_cache_layout(cache_k)
    vh0 = cache_v.reshape(Bs, WINDOW, KV_W)
    sample = _run_trunk(x_sample, (conv0, h0, kh0, vh0), weights, TT=Ts, pos0=PAST_LEN, n_kout=1,
                        name="trunk_sample")
    ys, sc, sr, sk, sv = unpack(*sample, Ts)

    return (yp, ys, pc, pr, pk, pv, sc, sr, sk, sv)
```

```python
import functools

import numpy as np
import jax
import jax.numpy as jnp
from jax import lax
from jax.experimental import pallas as pl
from jax.experimental.pallas import tpu as pltpu

D_MODEL = 1024
D_RNN = 1408
N_RG_BLOCKS = 16
RG_BLOCK = D_RNN // N_RG_BLOCKS
CONV_W = 4
LRU_C = 8.0
N_HEADS = 16
N_KV_HEADS = 4
HEAD_DIM = 64
HALF = HEAD_DIM // 2
GROUP = N_HEADS // N_KV_HEADS
WINDOW = 128
CHUNK = 64
PAST_LEN = 1024
ROPE_THETA = 10000.0
EPS = 1e-6

KV_W = N_KV_HEADS * HEAD_DIM
Q_W = N_HEADS * HEAD_DIM

LANES = 128
SUBLANES = 8
MXU_DIM = 256

NB = SUBLANES
TAIL = (CONV_W - 1) * NB

GATE_K = 512
N_GATE_TILES = -(-D_RNN // MXU_DIM)

KEY_WIN = 256

P_PRE_A, P_POST_A, P_KV, P_PRE_B, P_POST_B, P_INV_FREQ = range(6)
P_CONV_B, P_B_R, P_B_I, P_LAMBDA = range(CONV_W, CONV_W + 4)


def _gate_windows():
    wins = []
    for j in range(N_GATE_TILES):
        c0 = j * MXU_DIM
        c1 = min(c0 + MXU_DIM, D_RNN)
        lo = (c0 // RG_BLOCK) * RG_BLOCK
        hi = ((c1 - 1) // RG_BLOCK + 1) * RG_BLOCK
        k0 = min((lo // LANES) * LANES, D_RNN - GATE_K)
        assert k0 <= lo and hi <= k0 + GATE_K
        wins.append((k0, c0, c1))
    return wins


GATE_WINDOWS = _gate_windows()


def _bdot(a, b):
    return jnp.dot(a, b, preferred_element_type=jnp.float32)


def _sigmoid(x):
    return 1.0 / (1.0 + jnp.exp(-x))


def _trunk_kernel(*refs, TT, NT, NG, has_state, pos0, n_kout):
    it = iter(refs)
    x_hbm = next(it)
    if has_state:
        conv0_ref, h0_ref, kh0_ref, vh0_ref = next(it), next(it), next(it), next(it)
    p_model, p_rnn, w_in_a, w_ri, w_out_a, w_kv, w_in_b, sinks, w_out_b = [next(it) for _ in range(9)]
    y_hbm, tail_out, h_out, k_out, v_out = [next(it) for _ in range(5)]
    (in_buf, out_buf, sem_in, sem_out, ext_ref, tail_ref, hcar_ref, kbuf, vbuf,
     q_il, k_il, v_il, o_il, rope_small, cos_tab, sin_tab) = [next(it) for _ in range(16)]

    g_idx = pl.program_id(0)
    t_idx = pl.program_id(1)
    step = g_idx * NT + t_idx
    n_steps = NT * NG
    slot = step & 1
    R = TT * NB
    f32 = jnp.float32
    bf16 = jnp.bfloat16

    def in_copies(t, g, s):
        return [pltpu.make_async_copy(x_hbm.at[g * NB + b, pl.ds(t * TT, TT), :],
                                      in_buf.at[s, :, b, :], sem_in.at[s]) for b in range(NB)]

    def out_copies(t, g, s):
        return [pltpu.make_async_copy(out_buf.at[s, :, b, :],
                                      y_hbm.at[g * NB + b, pl.ds(t * TT, TT), :], sem_out.at[s]) for b in range(NB)]

    @pl.when(step == 0)
    def _():
        for c in in_copies(t_idx, g_idx, slot):
            c.start()

    @pl.when(step + 1 < n_steps)
    def _():
        nxt = step + 1
        for c in in_copies(nxt % NT, nxt // NT, 1 - slot):
            c.start()

    for c in in_copies(t_idx, g_idx, slot):
        c.wait()

    if has_state:
        @pl.when(t_idx == 0)
        def _():
            ext_ref[0:TAIL, :] = conv0_ref[...]
            hcar_ref[...] = h0_ref[...]
            zpad = jnp.zeros((NB, KEY_WIN - WINDOW, KV_W), bf16)
            kbuf[:, 0:WINDOW, :] = kh0_ref[...].astype(bf16)
            vbuf[:, 0:WINDOW, :] = vh0_ref[...].astype(bf16)
            kbuf[:, WINDOW:, :] = zpad
            vbuf[:, WINDOW:, :] = zpad
    else:
        @pl.when(t_idx == 0)
        def _():
            ext_ref[0:TAIL, :] = jnp.zeros((TAIL, D_RNN), f32)
            hcar_ref[...] = jnp.zeros((NB, D_RNN), f32)
            kbuf[...] = jnp.zeros((NB, KEY_WIN, KV_W), bf16)
            vbuf[...] = jnp.zeros((NB, KEY_WIN, KV_W), bf16)

    @pl.when(t_idx > 0)
    def _():
        ext_ref[0:TAIL, :] = tail_ref[...]

    f0 = pl.multiple_of(t_idx * TT, TT)

    @pl.when(g_idx == 0)
    def _():
        pos = (lax.broadcasted_iota(jnp.int32, (TT, LANES), 0) + (pos0 + t_idx * TT)).astype(f32)
        ang = pos * p_model[P_INV_FREQ:P_INV_FREQ + 1, :LANES]
        rope_small[0, pl.ds(f0, TT), :] = jnp.cos(ang)
        rope_small[1, pl.ds(f0, TT), :] = jnp.sin(ang)

    for t in range(TT):
        cos_tab[t * NB:(t + 1) * NB, :] = jnp.broadcast_to(rope_small[0, pl.ds(f0 + t, 1), :], (NB, LANES))
        sin_tab[t * NB:(t + 1) * NB, :] = jnp.broadcast_to(rope_small[1, pl.ds(f0 + t, 1), :], (NB, LANES))

    def rms_scale(v):
        return lax.rsqrt(jnp.sum(v * v, axis=-1, keepdims=True) * (1.0 / D_MODEL) + EPS)

    x = in_buf[slot].reshape(R, D_MODEL)
    xn = (x * rms_scale(x) * p_model[P_PRE_A:P_PRE_A + 1, :]).astype(bf16)
    u = _bdot(xn, w_in_a[...])
    branch = u[:, :D_RNN]
    gate = u[:, D_RNN:]

    ext_ref[TAIL:TAIL + R, :] = branch
    conv = p_rnn[P_CONV_B:P_CONV_B + 1, :] + branch * p_rnn[CONV_W - 1:CONV_W, :]
    for tap in range(CONV_W - 1):
        conv = conv + ext_ref[tap * NB:tap * NB + R, :] * p_rnn[tap:tap + 1, :]
    new_tail = branch[R - TAIL:, :]
    tail_ref[...] = new_tail
    tail_out[...] = new_tail

    conv_bf = conv.astype(bf16)
    r_parts, i_parts = [], []
    for j, (k0, c0, c1) in enumerate(GATE_WINDOWS):
        rg = _bdot(conv_bf[:, k0:k0 + GATE_K], w_ri[j])
        r_parts.append(rg[:, :c1 - c0])
        i_parts.append(rg[:, MXU_DIM:MXU_DIM + c1 - c0])
    r_gate = _sigmoid(jnp.concatenate(r_parts, axis=1) + p_rnn[P_B_R:P_B_R + 1, :])
    i_gate = _sigmoid(jnp.concatenate(i_parts, axis=1) + p_rnn[P_B_I:P_B_I + 1, :])

    z = -p_rnn[P_LAMBDA:P_LAMBDA + 1, :]
    softplus = jnp.maximum(z, 0.0) + jnp.log1p(jnp.exp(-jnp.abs(z)))
    log_a = (-LRU_C * softplus) * r_gate
    a = jnp.exp(log_a)
    bvec = jnp.sqrt(1.0 - a * a) * (i_gate * conv)

    h = hcar_ref[...]
    hs = []
    for t in range(TT):
        h = a[t * NB:(t + 1) * NB, :] * h + bvec[t * NB:(t + 1) * NB, :]
        hs.append(h)
    hcar_ref[...] = h
    h_out[...] = h
    h_all = jnp.concatenate(hs, axis=0)

    ya = _bdot((h_all * (gate * _sigmoid(gate))).astype(bf16), w_out_a[...])
    x1 = x + ya * rms_scale(ya) * p_model[P_POST_A:P_POST_A + 1, :]

    x1s = x1 * rms_scale(x1)
    kv = _bdot((x1s * p_model[P_KV:P_KV + 1, :]).astype(bf16), w_kv[...])
    ub = _bdot((x1s * p_model[P_PRE_B:P_PRE_B + 1, :]).astype(bf16), w_in_b[...])

    cos = cos_tab[...]
    sin = sin_tab[...]
    ka, kb, v = kv[:, :LANES], kv[:, LANES:2 * LANES], kv[:, 2 * LANES:]
    k_il[0] = ka * cos - kb * sin
    k_il[1] = kb * cos + ka * sin
    v_il[0] = v[:, :LANES]
    v_il[1] = v[:, LANES:]
    k_out[...] = jnp.concatenate([k_il[0], k_il[1]], axis=1)
    v_out[...] = v

    scale = HEAD_DIM ** -0.5
    for j in range(GROUP):
        qa = ub[:, j * LANES:(j + 1) * LANES]
        qb = ub[:, Q_W // 2 + j * LANES:Q_W // 2 + (j + 1) * LANES]
        q_il[j] = (qa * cos - qb * sin) * scale
        q_il[GROUP + j] = (qb * cos + qa * sin) * scale
    gate2 = ub[:, Q_W:]

    if has_state:
        cur_off = WINDOW
        n_valid = WINDOW + TT
    else:
        ring = WINDOW // TT + 1
        cur_off = pl.multiple_of((t_idx % ring) * TT, TT)
        n_valid = jnp.minimum(t_idx + 1, ring) * TT

    lane_q = lax.broadcasted_iota(jnp.int32, (TT, 2 * LANES), 1)
    q_slot = (lane_q & (LANES - 1)) // HALF
    o_slot = lane_q // HEAD_DIM
    invalid = lax.broadcasted_iota(jnp.int32, (TT, KEY_WIN), 1) >= n_valid
    zero_q = jnp.zeros((TT, 2 * LANES), bf16)

    def attend_stream(b, carry):
        rows = pl.ds(b, TT, stride=NB)
        kbuf[b, pl.ds(cur_off, TT), :] = jnp.concatenate(
            [k_il[0, rows, :], k_il[1, rows, :]], axis=1).astype(bf16)
        vbuf[b, pl.ds(cur_off, TT), :] = jnp.concatenate(
            [v_il[0, rows, :], v_il[1, rows, :]], axis=1).astype(bf16)
        qcat = [jnp.concatenate([q_il[j, rows, :], q_il[GROUP + j, rows, :]], axis=1).astype(bf16)
                for j in range(GROUP)]
        qstack = jnp.concatenate(
            [jnp.where(q_slot == h // GROUP, qcat[h % GROUP], zero_q) for h in range(N_HEADS)], axis=0)
        s_all = lax.dot_general(qstack, kbuf[b], (((1,), (1,)), ((), ())),
                                preferred_element_type=f32)
        p_parts, inv_l = [], []
        for h in range(N_HEADS):
            sink = sinks[h]
            sh = jnp.where(invalid, -jnp.inf, s_all[h * TT:(h + 1) * TT, :])
            mh = jnp.maximum(jnp.max(sh, axis=-1, keepdims=True), sink)
            ph = jnp.exp(sh - mh)
            lh = jnp.sum(ph, axis=-1, keepdims=True) + jnp.exp(sink - mh)
            p_parts.append(ph.astype(bf16))
            inv_l.append(1.0 / lh)
        o_all = _bdot(jnp.concatenate(p_parts, axis=0), vbuf[b])
        for j in range(GROUP):
            acc = jnp.zeros((TT, 2 * LANES), f32)
            for g in range(N_KV_HEADS):
                h = g * GROUP + j
                acc = jnp.where(o_slot == g, o_all[h * TT:(h + 1) * TT, :] * inv_l[h], acc)
            o_il[2 * j, rows, :] = acc[:, :LANES]
            o_il[2 * j + 1, rows, :] = acc[:, LANES:]
        return carry

    lax.fori_loop(0, NB, attend_stream, 0, unroll=4)

    o2 = jnp.concatenate([o_il[l] for l in range(Q_W // LANES)], axis=1)
    og = (o2 * (gate2 * _sigmoid(gate2))).astype(bf16)
    yb = _bdot(og, w_out_b[...])
    y = x1 + yb * rms_scale(yb) * p_model[P_POST_B:P_POST_B + 1, :]

    @pl.when(step >= 2)
    def _():
        for c in out_copies(t_idx, g_idx, slot):
            c.wait()

    out_buf[slot] = y.reshape(TT, NB, D_MODEL)
    for c in out_copies(t_idx, g_idx, slot):
        c.start()

    @pl.when(step == n_steps - 1)
    def _():
        if n_steps >= 2:
            for c in out_copies(t_idx, g_idx, 1 - slot):
                c.wait()
        for c in out_copies(t_idx, g_idx, slot):
            c.wait()


def _const_spec(shape):
    nd = len(shape)
    return pl.BlockSpec(shape, lambda g, t, _nd=nd: (0,) * _nd, pipeline_mode=pl.Buffered(1))


def _run_trunk(x, state, weights, *, TT, pos0, n_kout, name):
    B, S, _ = x.shape
    NT = S // TT
    NG = B // NB
    R = TT * NB
    assert NT * TT == S and NG * NB == B and n_kout <= NT
    assert state is not None or WINDOW % TT == 0
    has_state = state is not None

    in_specs = [pl.BlockSpec(memory_space=pl.ANY)]
    args = [x]
    if has_state:
        conv0, h0, kh0, vh0 = state
        in_specs += [
            pl.BlockSpec((None, TAIL, D_RNN), lambda g, t: (g, 0, 0)),
            pl.BlockSpec((None, NB, D_RNN), lambda g, t: (g, 0, 0)),
            pl.BlockSpec((NB, WINDOW, KV_W), lambda g, t: (g, 0, 0)),
            pl.BlockSpec((NB, WINDOW, KV_W), lambda g, t: (g, 0, 0)),
        ]
        args += [conv0, h0, kh0, vh0]
    for name_w, w in weights:
        if name_w == "sinks":
            in_specs.append(pl.BlockSpec(memory_space=pltpu.SMEM))
        else:
            in_specs.append(_const_spec(w.shape))
        args.append(w)

    kout_map = lambda g, t: (g, jnp.maximum(t - (NT - n_kout), 0), 0, 0)
    out_shape = (
        jax.ShapeDtypeStruct((B, S, D_MODEL), jnp.float32),
        jax.ShapeDtypeStruct((NG, TAIL, D_RNN), jnp.float32),
        jax.ShapeDtypeStruct((NG, NB, D_RNN), jnp.float32),
        jax.ShapeDtypeStruct((NG, n_kout, R, KV_W), jnp.float32),
        jax.ShapeDtypeStruct((NG, n_kout, R, KV_W), jnp.float32),
    )
    out_specs = (
        pl.BlockSpec(memory_space=pl.ANY),
        pl.BlockSpec((None, TAIL, D_RNN), lambda g, t: (g, 0, 0)),
        pl.BlockSpec((None, NB, D_RNN), lambda g, t: (g, 0, 0)),
        pl.BlockSpec((None, None, R, KV_W), kout_map),
        pl.BlockSpec((None, None, R, KV_W), kout_map),
    )
    scratch = [
        pltpu.VMEM((2, TT, NB, D_MODEL), jnp.float32),
        pltpu.VMEM((2, TT, NB, D_MODEL), jnp.float32),
        pltpu.SemaphoreType.DMA((2,)),
        pltpu.SemaphoreType.DMA((2,)),
        pltpu.VMEM((TAIL + R, D_RNN), jnp.float32),
        pltpu.VMEM((TAIL, D_RNN), jnp.float32),
        pltpu.VMEM((NB, D_RNN), jnp.float32),
        pltpu.VMEM((NB, KEY_WIN, KV_W), jnp.bfloat16),
        pltpu.VMEM((NB, KEY_WIN, KV_W), jnp.bfloat16),
        pltpu.VMEM((Q_W // LANES, R, LANES), jnp.float32),
        pltpu.VMEM((KV_W // LANES, R, LANES), jnp.float32),
        pltpu.VMEM((KV_W // LANES, R, LANES), jnp.float32),
        pltpu.VMEM((Q_W // LANES, R, LANES), jnp.float32),
        pltpu.VMEM((2, S, LANES), jnp.float32),
        pltpu.VMEM((R, LANES), jnp.float32),
        pltpu.VMEM((R, LANES), jnp.float32),
    ]
    kern = functools.partial(_trunk_kernel, TT=TT, NT=NT, NG=NG, has_state=has_state, pos0=pos0, n_kout=n_kout)
    return pl.pallas_call(
        kern,
        grid=(NG, NT),
        in_specs=in_specs,
        out_specs=out_specs,
        out_shape=out_shape,
        scratch_shapes=scratch,
        compiler_params=pltpu.CompilerParams(
            dimension_semantics=("arbitrary", "arbitrary"),
            vmem_limit_bytes=58 * 1024 * 1024),
        name=name,
    )(*args)


def _prep_weights(norm_pre_a, w_in_a, conv_w_a, conv_b_a, w_gate_r, b_gate_r, w_gate_i, b_gate_i,
                  lru_lambda, w_out_a, norm_post_a, norm_kv, w_kv, norm_pre_b, w_in_b, attn_sinks,
                  w_out_b, norm_post_b):
    bf16 = jnp.bfloat16
    f32 = jnp.float32

    w_g = jnp.stack([w_gate_r[0], w_gate_i[0]]).astype(bf16)
    dense = jnp.concatenate(
        [jnp.pad(w_g[:, n], ((0, 0), (0, 0), (RG_BLOCK * n, D_RNN - RG_BLOCK * (n + 1))))
         for n in range(N_RG_BLOCKS)], axis=1)
    tiles = []
    for k0, c0, c1 in GATE_WINDOWS:
        t = jnp.pad(dense[:, k0:k0 + GATE_K, c0:c1], ((0, 0), (0, 0), (0, MXU_DIM - (c1 - c0))))
        tiles.append(jnp.concatenate([t[0], t[1]], axis=1))
    w_ri = jnp.stack(tiles)

    wk = w_kv[:, :KV_W].astype(bf16).reshape(D_MODEL, N_KV_HEADS, 2, HALF)
    w_kv_p = jnp.concatenate([wk.transpose(0, 2, 1, 3).reshape(D_MODEL, KV_W), w_kv[:, KV_W:].astype(bf16)], axis=1)
    wb = w_in_b[0].astype(bf16)
    wq = wb[:, :Q_W].reshape(D_MODEL, N_KV_HEADS, GROUP, 2, HALF).transpose(0, 3, 2, 1, 4).reshape(D_MODEL, Q_W)
    wg = wb[:, Q_W:].reshape(D_MODEL, N_KV_HEADS, GROUP, HEAD_DIM).transpose(0, 2, 1, 3).reshape(D_MODEL, Q_W)
    w_in_b_p = jnp.concatenate([wq, wg], axis=1)
    w_out_b_p = w_out_b[0].astype(bf16).reshape(N_KV_HEADS, GROUP, HEAD_DIM, D_MODEL).transpose(1, 0, 2, 3).reshape(
        Q_W, D_MODEL)

    inv = ROPE_THETA ** (-jnp.arange(HALF, dtype=f32) / HALF)
    inv_row = jnp.pad(jnp.tile(inv, LANES // HALF), (0, D_MODEL - LANES))
    zeros_m = jnp.zeros((D_MODEL,), f32)
    p_model = jnp.stack([norm_pre_a[0], norm_post_a[0], norm_kv, norm_pre_b[0], norm_post_b[0], inv_row,
                         zeros_m, zeros_m]).astype(f32)
    p_rnn = jnp.concatenate([conv_w_a[0], conv_b_a, b_gate_r, b_gate_i, lru_lambda], axis=0).astype(f32)

    return [
        ("p_model", p_model),
        ("p_rnn", p_rnn),
        ("w_in_a", w_in_a[0].astype(bf16)),
        ("w_ri", w_ri),
        ("w_out_a", w_out_a[0].astype(bf16)),
        ("w_kv", w_kv_p),
        ("w_in_b", w_in_b_p),
        ("sinks", attn_sinks[0].astype(f32)),
        ("w_out_b", w_out_b_p),
    ]


def _k_to_cache_layout(k):
    B, R, _ = k.shape
    ka = k[..., :LANES].reshape(B, R, N_KV_HEADS, HALF)
    kb = k[..., LANES:].reshape(B, R, N_KV_HEADS, HALF)
    return jnp.concatenate([ka, kb], axis=-1)


def _k_from_cache_layout(k):
    B, R = k.shape[:2]
    return jnp.concatenate([k[..., :HALF].reshape(B, R, LANES), k[..., HALF:].reshape(B, R, LANES)], axis=-1)


def _to_streams(a, frames):
    NG, n, _, C = a.shape
    return a.reshape(NG, n, frames, NB, C).transpose(0, 3, 1, 2, 4).reshape(NG * NB, n * frames, C)


def kernel(x_prompt, x_sample, state_conv, state_rnn, cache_k, cache_v, norm_pre_a, w_in_a, conv_w_a, conv_b_a, w_gate_r, b_gate_r, w_gate_i, b_gate_i, lru_lambda, w_out_a, norm_post_a, norm_kv, w_kv, norm_pre_b, w_in_b, attn_sinks, w_out_b, norm_post_b):
    weights = _prep_weights(norm_pre_a, w_in_a, conv_w_a, conv_b_a, w_gate_r, b_gate_r, w_gate_i, b_gate_i,
                            lru_lambda, w_out_a, norm_post_a, norm_kv, w_kv, norm_pre_b, w_in_b,
                            attn_sinks, w_out_b, norm_post_b)
    tail = CONV_W - 1

    def unpack(y, t_out, h_out, k_o, v_o, frames):
        NG = t_out.shape[0]
        conv = t_out.reshape(NG, tail, NB, D_RNN).transpose(0, 2, 1, 3).reshape(NG * NB, tail, D_RNN)
        k = _k_to_cache_layout(_to_streams(k_o, frames))
        v = _to_streams(v_o, frames)
        return (y, conv[None], h_out.reshape(NG * NB, D_RNN)[None], k,
                v.reshape(v.shape[0], v.shape[1], N_KV_HEADS, HEAD_DIM))

    prompt = _run_trunk(x_prompt, None, weights, TT=CHUNK, pos0=0, n_kout=WINDOW // CHUNK, name="trunk_prompt")
    yp, pc, pr, pk, pv = unpack(*prompt, CHUNK)

    Bs, Ts, _ = x_sample.shape
    NGs = Bs // NB
    conv0 = state_conv[0].reshape(NGs, NB, tail, D_RNN).transpose(0, 2, 1, 3).reshape(NGs, TAIL, D_RNN)
    h0 = state_rnn[0].reshape(NGs, NB, D_RNN)
    kh0 = _k_from_cache_layout(cache_k)
    vh0 = cache_v.reshape(Bs, WINDOW, KV_W)
    sample = _run_trunk(x_sample, (conv0, h0, kh0, vh0), weights, TT=Ts, pos0=PAST_LEN, n_kout=1,
                        name="trunk_sample")
    ys, sc, sr, sk, sv = unpack(*sample, Ts)

    return (yp, ys, pc, pr, pk, pv, sc, sr, sk, sv)
```

```python
import functools

import numpy as np
import jax
import jax.numpy as jnp
from jax import lax
from jax.experimental import pallas as pl
from jax.experimental.pallas import tpu as pltpu

D_MODEL = 1024
D_RNN = 1408
N_RG_BLOCKS = 16
RG_BLOCK = D_RNN // N_RG_BLOCKS
CONV_W = 4
LRU_C = 8.0
N_HEADS = 16
N_KV_HEADS = 4
HEAD_DIM = 64
HALF = HEAD_DIM // 2
GROUP = N_HEADS // N_KV_HEADS
WINDOW = 128
CHUNK = 64
PAST_LEN = 1024
ROPE_THETA = 10000.0
EPS = 1e-6
LOG2E = 1.4426950408889634

KV_W = N_KV_HEADS * HEAD_DIM
Q_W = N_HEADS * HEAD_DIM

LANES = 128
SUBLANES = 8
MXU_DIM = 256

NB = SUBLANES
TAIL = (CONV_W - 1) * NB

GATE_K = 512
N_GATE_TILES = -(-D_RNN // MXU_DIM)

KEY_WIN = 256

P_PRE_A, P_POST_A, P_KV, P_PRE_B, P_POST_B, P_INV_FREQ = range(6)
P_CONV_B, P_B_R, P_B_I, P_LAMBDA = range(CONV_W, CONV_W + 4)


def _gate_windows():
    wins = []
    for j in range(N_GATE_TILES):
        c0 = j * MXU_DIM
        c1 = min(c0 + MXU_DIM, D_RNN)
        lo = (c0 // RG_BLOCK) * RG_BLOCK
        hi = ((c1 - 1) // RG_BLOCK + 1) * RG_BLOCK
        k0 = min((lo // LANES) * LANES, D_RNN - GATE_K)
        assert k0 <= lo and hi <= k0 + GATE_K
        wins.append((k0, c0, c1))
    return wins


GATE_WINDOWS = _gate_windows()


def _bdot(a, b):
    return jnp.dot(a, b, preferred_element_type=jnp.float32)


def _sigmoid(x):
    return 1.0 / (1.0 + jnp.exp(-x))


def _trunk_kernel(*refs, TT, NT, NG, has_state, pos0, n_kout):
    it = iter(refs)
    x_hbm = next(it)
    if has_state:
        conv0_ref, h0_ref, kh0_ref, vh0_ref = next(it), next(it), next(it), next(it)
    p_model, p_rnn, w_in_a, w_ri, w_out_a, w_kv, w_in_b, sinks, w_out_b = [next(it) for _ in range(9)]
    y_hbm, tail_out, h_out, k_out, v_out = [next(it) for _ in range(5)]
    (in_buf, out_buf, sem_in, sem_out, sem_kv, ext_ref, tail_ref, hcar_ref, kbuf, vbuf,
     q_il, k_il, v_il, o_il, rope_small, cos_tab, sin_tab) = [next(it) for _ in range(17)]

    g_idx = pl.program_id(0)
    t_idx = pl.program_id(1)
    step = g_idx * NT + t_idx
    n_steps = NT * NG
    slot = step & 1
    R = TT * NB
    f32 = jnp.float32
    bf16 = jnp.bfloat16

    def in_copies(t, g, s):
        return [pltpu.make_async_copy(x_hbm.at[g * NB + b, pl.ds(t * TT, TT), :],
                                      in_buf.at[s, :, b, :], sem_in.at[s]) for b in range(NB)]

    def out_copies(t, g):
        return [pltpu.make_async_copy(out_buf.at[:, b, :],
                                      y_hbm.at[g * NB + b, pl.ds(t * TT, TT), :], sem_out.at[0]) for b in range(NB)]

    @pl.when(step == 0)
    def _():
        for c in in_copies(t_idx, g_idx, slot):
            c.start()

    @pl.when(step + 1 < n_steps)
    def _():
        nxt = step + 1
        for c in in_copies(nxt % NT, nxt // NT, 1 - slot):
            c.start()

    for c in in_copies(t_idx, g_idx, slot):
        c.wait()

    if has_state:
        @pl.when(t_idx == 0)
        def _():
            ext_ref[0:TAIL, :] = conv0_ref[...]
            hcar_ref[...] = h0_ref[...]
            zpad = jnp.zeros((NB, KEY_WIN - WINDOW, KV_W), bf16)
            kbuf[:, 0:WINDOW, :] = kh0_ref[...].astype(bf16)
            vbuf[:, 0:WINDOW, :] = vh0_ref[...].astype(bf16)
            kbuf[:, WINDOW:, :] = zpad
            vbuf[:, WINDOW:, :] = zpad
    else:
        @pl.when(t_idx == 0)
        def _():
            ext_ref[0:TAIL, :] = jnp.zeros((TAIL, D_RNN), f32)
            hcar_ref[...] = jnp.zeros((NB, D_RNN), f32)
            kbuf[...] = jnp.zeros(kbuf.shape, bf16)
            vbuf[...] = jnp.zeros(vbuf.shape, bf16)

    @pl.when(t_idx > 0)
    def _():
        ext_ref[0:TAIL, :] = tail_ref[...]

    f0 = pl.multiple_of(t_idx * TT, TT)

    @pl.when(g_idx == 0)
    def _():
        pos = (lax.broadcasted_iota(jnp.int32, (TT, LANES), 0) + (pos0 + t_idx * TT)).astype(f32)
        ang = pos * p_model[P_INV_FREQ:P_INV_FREQ + 1, :LANES]
        rope_small[0, pl.ds(f0, TT), :] = jnp.cos(ang)
        rope_small[1, pl.ds(f0, TT), :] = jnp.sin(ang)

    for t in range(TT):
        cos_tab[t * NB:(t + 1) * NB, :] = jnp.broadcast_to(rope_small[0, pl.ds(f0 + t, 1), :], (NB, LANES))
        sin_tab[t * NB:(t + 1) * NB, :] = jnp.broadcast_to(rope_small[1, pl.ds(f0 + t, 1), :], (NB, LANES))

    def rms_scale(v):
        return lax.rsqrt(jnp.sum(v * v, axis=-1, keepdims=True) * (1.0 / D_MODEL) + EPS)

    x = in_buf[slot].reshape(R, D_MODEL)
    xn = (x * rms_scale(x) * p_model[P_PRE_A:P_PRE_A + 1, :]).astype(bf16)
    u = _bdot(xn, w_in_a[...])
    branch = u[:, :D_RNN]
    gate = u[:, D_RNN:]

    ext_ref[TAIL:TAIL + R, :] = branch
    conv = p_rnn[P_CONV_B:P_CONV_B + 1, :] + branch * p_rnn[CONV_W - 1:CONV_W, :]
    for tap in range(CONV_W - 1):
        conv = conv + ext_ref[tap * NB:tap * NB + R, :] * p_rnn[tap:tap + 1, :]
    new_tail = branch[R - TAIL:, :]
    tail_ref[...] = new_tail
    tail_out[...] = new_tail

    conv_bf = conv.astype(bf16)
    r_parts, i_parts = [], []
    for j, (k0, c0, c1) in enumerate(GATE_WINDOWS):
        rg = _bdot(conv_bf[:, k0:k0 + GATE_K], w_ri[j])
        r_parts.append(rg[:, :c1 - c0])
        i_parts.append(rg[:, MXU_DIM:MXU_DIM + c1 - c0])
    r_gate = _sigmoid(jnp.concatenate(r_parts, axis=1) + p_rnn[P_B_R:P_B_R + 1, :])
    i_gate = _sigmoid(jnp.concatenate(i_parts, axis=1) + p_rnn[P_B_I:P_B_I + 1, :])

    z = -p_rnn[P_LAMBDA:P_LAMBDA + 1, :]
    softplus = jnp.maximum(z, 0.0) + jnp.log1p(jnp.exp(-jnp.abs(z)))
    log_a = (-LRU_C * softplus) * r_gate
    a = jnp.exp(log_a)
    bvec = jnp.sqrt(1.0 - a * a) * (i_gate * conv)

    h = hcar_ref[...]
    hs = []
    for t in range(TT):
        h = a[t * NB:(t + 1) * NB, :] * h + bvec[t * NB:(t + 1) * NB, :]
        hs.append(h)
    hcar_ref[...] = h
    h_out[...] = h
    h_all = jnp.concatenate(hs, axis=0)

    ya = _bdot((h_all * (gate * _sigmoid(gate))).astype(bf16), w_out_a[...])
    x1 = x + ya * rms_scale(ya) * p_model[P_POST_A:P_POST_A + 1, :]

    x1s = x1 * rms_scale(x1)
    kv = _bdot((x1s * p_model[P_KV:P_KV + 1, :]).astype(bf16), w_kv[...])
    ub = _bdot((x1s * p_model[P_PRE_B:P_PRE_B + 1, :]).astype(bf16), w_in_b[...])

    cos = cos_tab[...]
    sin = sin_tab[...]
    ka, kb, v = kv[:, :LANES], kv[:, LANES:2 * LANES], kv[:, 2 * LANES:]
    k_il[0] = ka * cos - kb * sin
    k_il[1] = kb * cos + ka * sin
    v_il[0] = v[:, :LANES]
    v_il[1] = v[:, LANES:]

    @pl.when(t_idx >= NT - n_kout)
    def _():
        tt = t_idx - (NT - n_kout)
        cps = []
        for l in range(KV_W // LANES):
            cps.append(pltpu.make_async_copy(k_il.at[l], k_out.at[g_idx, tt, :, pl.ds(l * LANES, LANES)],
                                             sem_kv.at[0]))
            cps.append(pltpu.make_async_copy(v_il.at[l], v_out.at[g_idx, tt, :, pl.ds(l * LANES, LANES)],
                                             sem_kv.at[1]))
        for c in cps:
            c.start()
        for c in cps:
            c.wait()

    scale = HEAD_DIM ** -0.5 * LOG2E
    for j in range(GROUP):
        qa = ub[:, j * LANES:(j + 1) * LANES]
        qb = ub[:, Q_W // 2 + j * LANES:Q_W // 2 + (j + 1) * LANES]
        q_il[j] = (qa * cos - qb * sin) * scale
        q_il[GROUP + j] = (qb * cos + qa * sin) * scale
    gate2 = ub[:, Q_W:]

    if has_state:
        cur_off = WINDOW
        n_valid = WINDOW + TT
    else:
        ring = WINDOW // TT + 1
        cur_off = pl.multiple_of((t_idx % ring) * TT, TT)
        n_valid = jnp.minimum(t_idx + 1, ring) * TT

    lane_q = lax.broadcasted_iota(jnp.int32, (TT, 2 * LANES), 1)
    q_slot = (lane_q & (LANES - 1)) // HALF
    o_slot = lane_q // HEAD_DIM
    invalid = lax.broadcasted_iota(jnp.int32, (TT, KEY_WIN), 1) >= n_valid
    col_row = lax.broadcasted_iota(jnp.int32, (1, KEY_WIN), 1)

    def attend_stream(b, carry):
        rows = pl.ds(b, TT, stride=NB)
        kbuf[b, pl.ds(cur_off, TT), :] = jnp.concatenate(
            [k_il[0, rows, :], k_il[1, rows, :]], axis=1).astype(bf16)
        vbuf[b, pl.ds(cur_off, TT), :] = jnp.concatenate(
            [v_il[0, rows, :], v_il[1, rows, :]], axis=1).astype(bf16)
        qcat = [jnp.concatenate([q_il[j, rows, :], q_il[GROUP + j, rows, :]], axis=1) for j in range(GROUP)]
        qstack = jnp.concatenate(
            [jnp.where(q_slot == h // GROUP, qcat[h % GROUP], 0.0) for h in range(N_HEADS)],
            axis=0).astype(bf16)
        s_all = lax.dot_general(qstack, kbuf[b], (((1,), (1,)), ((), ())),
                                preferred_element_type=f32)
        p_parts, inv_l = [], []
        for h in range(N_HEADS):
            fill = jnp.where(col_row == n_valid, sinks[h] * LOG2E, -jnp.inf)
            sh = jnp.where(invalid, fill, s_all[h * TT:(h + 1) * TT, :])
            ph = jnp.exp2(sh - jnp.max(sh, axis=-1, keepdims=True))
            inv_l.append(1.0 / jnp.sum(ph, axis=-1, keepdims=True))
            p_parts.append(ph.astype(bf16))
        o_all = _bdot(jnp.concatenate(p_parts, axis=0), vbuf[b])
        for j in range(GROUP):
            acc = o_all[j * TT:(j + 1) * TT, :] * inv_l[j]
            for g in range(1, N_KV_HEADS):
                h = g * GROUP + j
                acc = jnp.where(o_slot == g, o_all[h * TT:(h + 1) * TT, :] * inv_l[h], acc)
            o_il[2 * j, rows, :] = acc[:, :LANES]
            o_il[2 * j + 1, rows, :] = acc[:, LANES:]
        return carry

    lax.fori_loop(0, NB, attend_stream, 0, unroll=4)

    o2 = jnp.concatenate([o_il[l] for l in range(Q_W // LANES)], axis=1)
    og = (o2 * (gate2 * _sigmoid(gate2))).astype(bf16)
    yb = _bdot(og, w_out_b[...])
    y = x1 + yb * rms_scale(yb) * p_model[P_POST_B:P_POST_B + 1, :]

    @pl.when(step >= 1)
    def _():
        for c in out_copies(t_idx, g_idx):
            c.wait()

    out_buf[...] = y.reshape(TT, NB, D_MODEL)
    for c in out_copies(t_idx, g_idx):
        c.start()

    @pl.when(step == n_steps - 1)
    def _():
        for c in out_copies(t_idx, g_idx):
            c.wait()


def _const_spec(shape):
    nd = len(shape)
    return pl.BlockSpec(shape, lambda g, t, _nd=nd: (0,) * _nd, pipeline_mode=pl.Buffered(1))


def _run_trunk(x, state, weights, *, TT, pos0, n_kout, name):
    B, S, _ = x.shape
    NT = S // TT
    NG = B // NB
    R = TT * NB
    assert NT * TT == S and NG * NB == B and n_kout <= NT
    assert state is not None or WINDOW % TT == 0
    has_state = state is not None

    in_specs = [pl.BlockSpec(memory_space=pl.ANY)]
    args = [x]
    if has_state:
        conv0, h0, kh0, vh0 = state
        in_specs += [
            pl.BlockSpec((None, TAIL, D_RNN), lambda g, t: (g, 0, 0)),
            pl.BlockSpec((None, NB, D_RNN), lambda g, t: (g, 0, 0)),
            pl.BlockSpec((NB, WINDOW, KV_W), lambda g, t: (g, 0, 0)),
            pl.BlockSpec((NB, WINDOW, KV_W), lambda g, t: (g, 0, 0)),
        ]
        args += [conv0, h0, kh0, vh0]
    for name_w, w in weights:
        if name_w == "sinks":
            in_specs.append(pl.BlockSpec(memory_space=pltpu.SMEM))
        else:
            in_specs.append(_const_spec(w.shape))
        args.append(w)

    out_shape = (
        jax.ShapeDtypeStruct((B, S, D_MODEL), jnp.float32),
        jax.ShapeDtypeStruct((NG, TAIL, D_RNN), jnp.float32),
        jax.ShapeDtypeStruct((NG, NB, D_RNN), jnp.float32),
        jax.ShapeDtypeStruct((NG, n_kout, R, KV_W), jnp.float32),
        jax.ShapeDtypeStruct((NG, n_kout, R, KV_W), jnp.float32),
    )
    out_specs = (
        pl.BlockSpec(memory_space=pl.ANY),
        pl.BlockSpec((None, TAIL, D_RNN), lambda g, t: (g, 0, 0)),
        pl.BlockSpec((None, NB, D_RNN), lambda g, t: (g, 0, 0)),
        pl.BlockSpec(memory_space=pl.ANY),
        pl.BlockSpec(memory_space=pl.ANY),
    )
    scratch = [
        pltpu.VMEM((2, TT, NB, D_MODEL), jnp.float32),
        pltpu.VMEM((TT, NB, D_MODEL), jnp.float32),
        pltpu.SemaphoreType.DMA((2,)),
        pltpu.SemaphoreType.DMA((1,)),
        pltpu.SemaphoreType.DMA((2,)),
        pltpu.VMEM((TAIL + R, D_RNN), jnp.float32),
        pltpu.VMEM((TAIL, D_RNN), jnp.float32),
        pltpu.VMEM((NB, D_RNN), jnp.float32),
        pltpu.VMEM((NB, KEY_WIN, KV_W), jnp.bfloat16),
        pltpu.VMEM((NB, KEY_WIN, KV_W), jnp.bfloat16),
        pltpu.VMEM((Q_W // LANES, R, LANES), jnp.float32),
        pltpu.VMEM((KV_W // LANES, R, LANES), jnp.float32),
        pltpu.VMEM((KV_W // LANES, R, LANES), jnp.float32),
        pltpu.VMEM((Q_W // LANES, R, LANES), jnp.float32),
        pltpu.VMEM((2, S, LANES), jnp.float32),
        pltpu.VMEM((R, LANES), jnp.float32),
        pltpu.VMEM((R, LANES), jnp.float32),
    ]
    kern = functools.partial(_trunk_kernel, TT=TT, NT=NT, NG=NG, has_state=has_state, pos0=pos0, n_kout=n_kout)
    return pl.pallas_call(
        kern,
        grid=(NG, NT),
        in_specs=in_specs,
        out_specs=out_specs,
        out_shape=out_shape,
        scratch_shapes=scratch,
        compiler_params=pltpu.CompilerParams(
            dimension_semantics=("arbitrary", "arbitrary"),
            vmem_limit_bytes=58 * 1024 * 1024),
        name=name,
    )(*args)


def _prep_weights(norm_pre_a, w_in_a, conv_w_a, conv_b_a, w_gate_r, b_gate_r, w_gate_i, b_gate_i,
                  lru_lambda, w_out_a, norm_post_a, norm_kv, w_kv, norm_pre_b, w_in_b, attn_sinks,
                  w_out_b, norm_post_b):
    bf16 = jnp.bfloat16
    f32 = jnp.float32

    w_g = jnp.stack([w_gate_r[0], w_gate_i[0]]).astype(bf16)
    dense = jnp.concatenate(
        [jnp.pad(w_g[:, n], ((0, 0), (0, 0), (RG_BLOCK * n, D_RNN - RG_BLOCK * (n + 1))))
         for n in range(N_RG_BLOCKS)], axis=1)
    tiles = []
    for k0, c0, c1 in GATE_WINDOWS:
        t = jnp.pad(dense[:, k0:k0 + GATE_K, c0:c1], ((0, 0), (0, 0), (0, MXU_DIM - (c1 - c0))))
        tiles.append(jnp.concatenate([t[0], t[1]], axis=1))
    w_ri = jnp.stack(tiles)

    wk = w_kv[:, :KV_W].astype(bf16).reshape(D_MODEL, N_KV_HEADS, 2, HALF)
    w_kv_p = jnp.concatenate([wk.transpose(0, 2, 1, 3).reshape(D_MODEL, KV_W), w_kv[:, KV_W:].astype(bf16)], axis=1)
    wb = w_in_b[0].astype(bf16)
    wq = wb[:, :Q_W].reshape(D_MODEL, N_KV_HEADS, GROUP, 2, HALF).transpose(0, 3, 2, 1, 4).reshape(D_MODEL, Q_W)
    wg = wb[:, Q_W:].reshape(D_MODEL, N_KV_HEADS, GROUP, HEAD_DIM).transpose(0, 2, 1, 3).reshape(D_MODEL, Q_W)
    w_in_b_p = jnp.concatenate([wq, wg], axis=1)
    w_out_b_p = w_out_b[0].astype(bf16).reshape(N_KV_HEADS, GROUP, HEAD_DIM, D_MODEL).transpose(1, 0, 2, 3).reshape(
        Q_W, D_MODEL)

    inv = ROPE_THETA ** (-jnp.arange(HALF, dtype=f32) / HALF)
    inv_row = jnp.pad(jnp.tile(inv, LANES // HALF), (0, D_MODEL - LANES))
    zeros_m = jnp.zeros((D_MODEL,), f32)
    p_model = jnp.stack([norm_pre_a[0], norm_post_a[0], norm_kv, norm_pre_b[0], norm_post_b[0], inv_row,
                         zeros_m, zeros_m]).astype(f32)
    p_rnn = jnp.concatenate([conv_w_a[0], conv_b_a, b_gate_r, b_gate_i, lru_lambda], axis=0).astype(f32)

    return [
        ("p_model", p_model),
        ("p_rnn", p_rnn),
        ("w_in_a", w_in_a[0].astype(bf16)),
        ("w_ri", w_ri),
        ("w_out_a", w_out_a[0].astype(bf16)),
        ("w_kv", w_kv_p),
        ("w_in_b", w_in_b_p),
        ("sinks", attn_sinks[0].astype(f32)),
        ("w_out_b", w_out_b_p),
    ]


def _k_to_cache_layout(k):
    B, R, _ = k.shape
    ka = k[..., :LANES].reshape(B, R, N_KV_HEADS, HALF)
    kb = k[..., LANES:].reshape(B, R, N_KV_HEADS, HALF)
    return jnp.concatenate([ka, kb], axis=-1)


def _k_from_cache_layout(k):
    B, R = k.shape[:2]
    return jnp.concatenate([k[..., :HALF].reshape(B, R, LANES), k[..., HALF:].reshape(B, R, LANES)], axis=-1)


def _to_streams(a, frames):
    NG, n, _, C = a.shape
    return a.reshape(NG, n, frames, NB, C).transpose(0, 3, 1, 2, 4).reshape(NG * NB, n * frames, C)


def kernel(x_prompt, x_sample, state_conv, state_rnn, cache_k, cache_v, norm_pre_a, w_in_a, conv_w_a, conv_b_a, w_gate_r, b_gate_r, w_gate_i, b_gate_i, lru_lambda, w_out_a, norm_post_a, norm_kv, w_kv, norm_pre_b, w_in_b, attn_sinks, w_out_b, norm_post_b):
    weights = _prep_weights(norm_pre_a, w_in_a, conv_w_a, conv_b_a, w_gate_r, b_gate_r, w_gate_i, b_gate_i,
                            lru_lambda, w_out_a, norm_post_a, norm_kv, w_kv, norm_pre_b, w_in_b,
                            attn_sinks, w_out_b, norm_post_b)
    tail = CONV_W - 1

    def unpack(y, t_out, h_out, k_o, v_o, frames):
        NG = t_out.shape[0]
        conv = t_out.reshape(NG, tail, NB, D_RNN).transpose(0, 2, 1, 3).reshape(NG * NB, tail, D_RNN)
        k = _k_to_cache_layout(_to_streams(k_o, frames))
        v = _to_streams(v_o, frames)
        return (y, conv[None], h_out.reshape(NG * NB, D_RNN)[None], k,
                v.reshape(v.shape[0], v.shape[1], N_KV_HEADS, HEAD_DIM))

    prompt = _run_trunk(x_prompt, None, weights, TT=CHUNK, pos0=0, n_kout=WINDOW // CHUNK, name="trunk_prompt")
    yp, pc, pr, pk, pv = unpack(*prompt, CHUNK)

    Bs, Ts, _ = x_sample.shape
    NGs = Bs // NB
    conv0 = state_conv[0].reshape(NGs, NB, tail, D_RNN).transpose(0, 2, 1, 3).reshape(NGs, TAIL, D_RNN)
    h0 = state_rnn[0].reshape(NGs, NB, D_RNN)
    kh0 = _k_from_cache_layout(cache_k)
    vh0 = cache_v.reshape(Bs, WINDOW, KV_W)
    sample = _run_trunk(x_sample, (conv0, h0, kh0, vh0), weights, TT=Ts, pos0=PAST_LEN, n_kout=1,
                        name="trunk_sample")
    ys, sc, sr, sk, sv = unpack(*sample, Ts)

    return (yp, ys, pc, pr, pk, pv, sc, sr, sk, sv)
```

```python
import functools

import numpy as np
import jax
import jax.numpy as jnp
from jax import lax
from jax.experimental import pallas as pl
from jax.experimental.pallas import tpu as pltpu

D_MODEL = 1024
D_RNN = 1408
N_RG_BLOCKS = 16
RG_BLOCK = D_RNN // N_RG_BLOCKS
CONV_W = 4
LRU_C = 8.0
N_HEADS = 16
N_KV_HEADS = 4
HEAD_DIM = 64
HALF = HEAD_DIM // 2
GROUP = N_HEADS // N_KV_HEADS
WINDOW = 128
CHUNK = 64
PAST_LEN = 1024
ROPE_THETA = 10000.0
EPS = 1e-6
LOG2E = 1.4426950408889634

KV_W = N_KV_HEADS * HEAD_DIM
Q_W = N_HEADS * HEAD_DIM

LANES = 128
SUBLANES = 8
MXU_DIM = 256

NB = SUBLANES
TAIL = (CONV_W - 1) * NB

GATE_K = 512
N_GATE_TILES = -(-D_RNN // MXU_DIM)

KEY_WIN = 256
N_PARTS = 1
W_PAD = LANES

P_PRE_A, P_POST_A, P_KV, P_PRE_B, P_POST_B, P_INV_FREQ = range(6)
P_CONV_B, P_B_R, P_B_I, P_LAMBDA = range(CONV_W, CONV_W + 4)


def _gate_windows():
    wins = []
    for j in range(N_GATE_TILES):
        c0 = j * MXU_DIM
        c1 = min(c0 + MXU_DIM, D_RNN)
        lo = (c0 // RG_BLOCK) * RG_BLOCK
        hi = ((c1 - 1) // RG_BLOCK + 1) * RG_BLOCK
        k0 = min((lo // LANES) * LANES, D_RNN - GATE_K)
        assert k0 <= lo and hi <= k0 + GATE_K
        wins.append((k0, c0, c1))
    return wins


GATE_WINDOWS = _gate_windows()


def _bdot(a, b):
    return jnp.dot(a, b, preferred_element_type=jnp.float32)


def _sigmoid(x):
    return 1.0 / (1.0 + jnp.exp(-x))


def _trunk_kernel(*refs, TT, NT, NG, has_state, pos0, n_kout, n_parts):
    it = iter(refs)
    x_hbm = next(it)
    if has_state:
        conv0_ref, h0_ref, kh0_ref, vh0_ref = next(it), next(it), next(it), next(it)
    p_model, p_rnn, w_in_a, w_ri, w_out_a, w_kv, w_in_b, sinks, w_out_b = [next(it) for _ in range(9)]
    y_hbm, tail_out, h_out, k_out, v_out = [next(it) for _ in range(5)]
    (in_buf, out_buf, sem_in, sem_out, sem_kv, ext_ref, tail_ref, hcar_ref, kbuf, vbuf,
     q_il, k_il, v_il, o_il, rope_small, cos_tab, sin_tab) = [next(it) for _ in range(17)]

    g_idx = pl.program_id(0)
    t_idx = pl.program_id(1)
    step = g_idx * NT + t_idx
    n_steps = NT * NG
    slot = step & 1
    R = TT * NB
    f32 = jnp.float32
    bf16 = jnp.bfloat16

    def in_copies(t, g, s):
        return [pltpu.make_async_copy(x_hbm.at[g * NB + b, pl.ds(t * TT, TT), :],
                                      in_buf.at[s, :, b, :], sem_in.at[s]) for b in range(NB)]

    def out_copies(t, g):
        return [pltpu.make_async_copy(out_buf.at[:, b, :],
                                      y_hbm.at[g * NB + b, pl.ds(t * TT, TT), :], sem_out.at[0]) for b in range(NB)]

    @pl.when(step == 0)
    def _():
        for c in in_copies(t_idx, g_idx, slot):
            c.start()

    @pl.when(step + 1 < n_steps)
    def _():
        nxt = step + 1
        for c in in_copies(nxt % NT, nxt // NT, 1 - slot):
            c.start()

    for c in in_copies(t_idx, g_idx, slot):
        c.wait()

    if has_state:
        @pl.when(t_idx == 0)
        def _():
            ext_ref[0:TAIL, :] = conv0_ref[...]
            hcar_ref[...] = h0_ref[...]
            zpad = jnp.zeros((NB, KEY_WIN - WINDOW, KV_W), bf16)
            kbuf[:, 0:WINDOW, :] = kh0_ref[...].astype(bf16)
            vbuf[:, 0:WINDOW, :] = vh0_ref[...].astype(bf16)
            kbuf[:, WINDOW:, :] = zpad
            vbuf[:, WINDOW:, :] = zpad
    else:
        @pl.when(t_idx == 0)
        def _():
            ext_ref[0:TAIL, :] = jnp.zeros((TAIL, D_RNN), f32)
            hcar_ref[...] = jnp.zeros((NB, D_RNN), f32)
            kbuf[...] = jnp.zeros(kbuf.shape, bf16)
            vbuf[...] = jnp.zeros(vbuf.shape, bf16)

    @pl.when(t_idx > 0)
    def _():
        ext_ref[0:TAIL, :] = tail_ref[...]

    f0 = pl.multiple_of(t_idx * TT, TT)

    @pl.when(g_idx == 0)
    def _():
        pos = (lax.broadcasted_iota(jnp.int32, (TT, LANES), 0) + (pos0 + t_idx * TT)).astype(f32)
        ang = pos * p_model[P_INV_FREQ:P_INV_FREQ + 1, :LANES]
        rope_small[0, pl.ds(f0, TT), :] = jnp.cos(ang)
        rope_small[1, pl.ds(f0, TT), :] = jnp.sin(ang)

    for t in range(TT):
        cos_tab[t * NB:(t + 1) * NB, :] = jnp.broadcast_to(rope_small[0, pl.ds(f0 + t, 1), :], (NB, LANES))
        sin_tab[t * NB:(t + 1) * NB, :] = jnp.broadcast_to(rope_small[1, pl.ds(f0 + t, 1), :], (NB, LANES))

    def rms_scale(v):
        return lax.rsqrt(jnp.sum(v * v, axis=-1, keepdims=True) * (1.0 / D_MODEL) + EPS)

    FP = TT // n_parts
    RP = FP * NB
    scale = HEAD_DIM ** -0.5 * LOG2E
    z = -p_rnn[P_LAMBDA:P_LAMBDA + 1, :]
    softplus = jnp.maximum(z, 0.0) + jnp.log1p(jnp.exp(-jnp.abs(z)))
    h = hcar_ref[...]
    x1_parts, gate2_parts = [], []
    for part in range(n_parts):
        r0 = part * RP

        x = in_buf[slot, part * FP:(part + 1) * FP].reshape(RP, D_MODEL)
        xn = (x * rms_scale(x) * p_model[P_PRE_A:P_PRE_A + 1, :]).astype(bf16)
        u = _bdot(xn, w_in_a[...])
        branch = u[:, :D_RNN]
        gate = u[:, D_RNN:]

        ext_ref[TAIL + r0:TAIL + r0 + RP, :] = branch
        conv = p_rnn[P_CONV_B:P_CONV_B + 1, :] + branch * p_rnn[CONV_W - 1:CONV_W, :]
        for tap in range(CONV_W - 1):
            conv = conv + ext_ref[tap * NB + r0:tap * NB + r0 + RP, :] * p_rnn[tap:tap + 1, :]
        if part == n_parts - 1:
            new_tail = branch[RP - TAIL:, :]
            tail_ref[...] = new_tail
            tail_out[...] = new_tail

        conv_bf = conv.astype(bf16)
        r_parts, i_parts = [], []
        for j, (k0, c0, c1) in enumerate(GATE_WINDOWS):
            rg = _bdot(conv_bf[:, k0:k0 + GATE_K], w_ri[j])
            r_parts.append(rg[:, :c1 - c0])
            i_parts.append(rg[:, MXU_DIM:MXU_DIM + c1 - c0])
        r_gate = _sigmoid(jnp.concatenate(r_parts, axis=1) + p_rnn[P_B_R:P_B_R + 1, :])
        i_gate = _sigmoid(jnp.concatenate(i_parts, axis=1) + p_rnn[P_B_I:P_B_I + 1, :])

        a = jnp.exp2((-LRU_C * LOG2E * softplus) * r_gate)
        one_m_a2 = 1.0 - a * a
        root = jnp.where(one_m_a2 > 0.0, one_m_a2 * lax.rsqrt(one_m_a2), 0.0)
        bvec = root * (i_gate * conv)

        hs = []
        for t in range(FP):
            h = a[t * NB:(t + 1) * NB, :] * h + bvec[t * NB:(t + 1) * NB, :]
            hs.append(h)
        h_all = jnp.concatenate(hs, axis=0)

        ya = _bdot((h_all * (gate * _sigmoid(gate))).astype(bf16), w_out_a[:, :D_MODEL])
        x1 = x + ya * rms_scale(ya) * p_model[P_POST_A:P_POST_A + 1, :]
        x1_parts.append(x1)

        x1s = x1 * rms_scale(x1)
        kv = _bdot((x1s * p_model[P_KV:P_KV + 1, :]).astype(bf16), w_kv[...])
        ub = _bdot((x1s * p_model[P_PRE_B:P_PRE_B + 1, :]).astype(bf16), w_in_b[:, :2 * Q_W])

        cos = cos_tab[r0:r0 + RP, :]
        sin = sin_tab[r0:r0 + RP, :]
        ka, kb, v = kv[:, :LANES], kv[:, LANES:2 * LANES], kv[:, 2 * LANES:]
        k_il[0, r0:r0 + RP, :] = ka * cos - kb * sin
        k_il[1, r0:r0 + RP, :] = kb * cos + ka * sin
        v_il[0, r0:r0 + RP, :] = v[:, :LANES]
        v_il[1, r0:r0 + RP, :] = v[:, LANES:]
        for j in range(GROUP):
            qa = ub[:, j * LANES:(j + 1) * LANES]
            qb = ub[:, Q_W // 2 + j * LANES:Q_W // 2 + (j + 1) * LANES]
            q_il[j, r0:r0 + RP, :] = (qa * cos - qb * sin) * scale
            q_il[GROUP + j, r0:r0 + RP, :] = (qb * cos + qa * sin) * scale
        gate2_parts.append(ub[:, Q_W:])
    hcar_ref[...] = h
    h_out[...] = h

    @pl.when(t_idx >= NT - n_kout)
    def _():
        tt = t_idx - (NT - n_kout)
        cps = []
        for l in range(KV_W // LANES):
            cps.append(pltpu.make_async_copy(k_il.at[l], k_out.at[g_idx, tt, :, pl.ds(l * LANES, LANES)],
                                             sem_kv.at[0]))
            cps.append(pltpu.make_async_copy(v_il.at[l], v_out.at[g_idx, tt, :, pl.ds(l * LANES, LANES)],
                                             sem_kv.at[1]))
        for c in cps:
            c.start()
        for c in cps:
            c.wait()

    if has_state:
        cur_off = WINDOW
        n_valid = WINDOW + TT
    else:
        ring = WINDOW // TT + 1
        cur_off = pl.multiple_of((t_idx % ring) * TT, TT)
        n_valid = jnp.minimum(t_idx + 1, ring) * TT

    lane_q = lax.broadcasted_iota(jnp.int32, (TT, 2 * LANES), 1)
    q_slot = (lane_q & (LANES - 1)) // HALF
    o_slot = lane_q // HEAD_DIM
    invalid = lax.broadcasted_iota(jnp.int32, (TT, KEY_WIN), 1) >= n_valid
    col_row = lax.broadcasted_iota(jnp.int32, (1, KEY_WIN), 1)

    def attend_stream(b, carry):
        rows = pl.ds(b, TT, stride=NB)
        kbuf[b, pl.ds(cur_off, TT), :] = jnp.concatenate(
            [k_il[0, rows, :], k_il[1, rows, :]], axis=1).astype(bf16)
        vbuf[b, pl.ds(cur_off, TT), :] = jnp.concatenate(
            [v_il[0, rows, :], v_il[1, rows, :]], axis=1).astype(bf16)
        qcat = [jnp.concatenate([q_il[j, rows, :], q_il[GROUP + j, rows, :]], axis=1) for j in range(GROUP)]
        qstack = jnp.concatenate(
            [jnp.where(q_slot == h // GROUP, qcat[h % GROUP], 0.0) for h in range(N_HEADS)],
            axis=0).astype(bf16)
        s_all = lax.dot_general(qstack, kbuf[b], (((1,), (1,)), ((), ())),
                                preferred_element_type=f32)
        p_parts, inv_l = [], []
        for h in range(N_HEADS):
            fill = jnp.where(col_row == n_valid, sinks[h] * LOG2E, -jnp.inf)
            sh = jnp.where(invalid, fill, s_all[h * TT:(h + 1) * TT, :])
            ph = jnp.exp2(sh - jnp.max(sh, axis=-1, keepdims=True))
            inv_l.append(1.0 / jnp.sum(ph, axis=-1, keepdims=True))
            p_parts.append(ph.astype(bf16))
        o_all = _bdot(jnp.concatenate(p_parts, axis=0), vbuf[b])
        for j in range(GROUP):
            acc = o_all[j * TT:(j + 1) * TT, :] * inv_l[j]
            for g in range(1, N_KV_HEADS):
                h = g * GROUP + j
                acc = jnp.where(o_slot == g, o_all[h * TT:(h + 1) * TT, :] * inv_l[h], acc)
            o_il[2 * j, rows, :] = acc[:, :LANES]
            o_il[2 * j + 1, rows, :] = acc[:, LANES:]
        return carry

    lax.fori_loop(0, NB, attend_stream, 0, unroll=4)

    @pl.when(step >= 1)
    def _():
        for c in out_copies(t_idx, g_idx):
            c.wait()

    for part in range(n_parts):
        r0 = part * RP
        o2 = jnp.concatenate([o_il[l, r0:r0 + RP, :] for l in range(Q_W // LANES)], axis=1)
        gate2 = gate2_parts[part]
        og = (o2 * (gate2 * _sigmoid(gate2))).astype(bf16)
        yb = _bdot(og, w_out_b[:, :D_MODEL])
        y = x1_parts[part] + yb * rms_scale(yb) * p_model[P_POST_B:P_POST_B + 1, :]
        out_buf[part * FP:(part + 1) * FP] = y.reshape(FP, NB, D_MODEL)
    for c in out_copies(t_idx, g_idx):
        c.start()

    @pl.when(step == n_steps - 1)
    def _():
        for c in out_copies(t_idx, g_idx):
            c.wait()


def _const_spec(shape):
    nd = len(shape)
    return pl.BlockSpec(shape, lambda g, t, _nd=nd: (0,) * _nd, pipeline_mode=pl.Buffered(1))


def _run_trunk(x, state, weights, *, TT, pos0, n_kout, name):
    B, S, _ = x.shape
    NT = S // TT
    NG = B // NB
    R = TT * NB
    assert NT * TT == S and NG * NB == B and n_kout <= NT
    assert state is not None or WINDOW % TT == 0
    has_state = state is not None

    in_specs = [pl.BlockSpec(memory_space=pl.ANY)]
    args = [x]
    if has_state:
        conv0, h0, kh0, vh0 = state
        in_specs += [
            pl.BlockSpec((None, TAIL, D_RNN), lambda g, t: (g, 0, 0)),
            pl.BlockSpec((None, NB, D_RNN), lambda g, t: (g, 0, 0)),
            pl.BlockSpec((NB, WINDOW, KV_W), lambda g, t: (g, 0, 0)),
            pl.BlockSpec((NB, WINDOW, KV_W), lambda g, t: (g, 0, 0)),
        ]
        args += [conv0, h0, kh0, vh0]
    for name_w, w in weights:
        if name_w == "sinks":
            in_specs.append(pl.BlockSpec(memory_space=pltpu.SMEM))
        else:
            in_specs.append(_const_spec(w.shape))
        args.append(w)

    out_shape = (
        jax.ShapeDtypeStruct((B, S, D_MODEL), jnp.float32),
        jax.ShapeDtypeStruct((NG, TAIL, D_RNN), jnp.float32),
        jax.ShapeDtypeStruct((NG, NB, D_RNN), jnp.float32),
        jax.ShapeDtypeStruct((NG, n_kout, R, KV_W), jnp.float32),
        jax.ShapeDtypeStruct((NG, n_kout, R, KV_W), jnp.float32),
    )
    out_specs = (
        pl.BlockSpec(memory_space=pl.ANY),
        pl.BlockSpec((None, TAIL, D_RNN), lambda g, t: (g, 0, 0)),
        pl.BlockSpec((None, NB, D_RNN), lambda g, t: (g, 0, 0)),
        pl.BlockSpec(memory_space=pl.ANY),
        pl.BlockSpec(memory_space=pl.ANY),
    )
    scratch = [
        pltpu.VMEM((2, TT, NB, D_MODEL), jnp.float32),
        pltpu.VMEM((TT, NB, D_MODEL), jnp.float32),
        pltpu.SemaphoreType.DMA((2,)),
        pltpu.SemaphoreType.DMA((1,)),
        pltpu.SemaphoreType.DMA((2,)),
        pltpu.VMEM((TAIL + R, D_RNN), jnp.float32),
        pltpu.VMEM((TAIL, D_RNN), jnp.float32),
        pltpu.VMEM((NB, D_RNN), jnp.float32),
        pltpu.VMEM((NB, KEY_WIN, KV_W), jnp.bfloat16),
        pltpu.VMEM((NB, KEY_WIN, KV_W), jnp.bfloat16),
        pltpu.VMEM((Q_W // LANES, R, LANES), jnp.float32),
        pltpu.VMEM((KV_W // LANES, R, LANES), jnp.float32),
        pltpu.VMEM((KV_W // LANES, R, LANES), jnp.float32),
        pltpu.VMEM((Q_W // LANES, R, LANES), jnp.float32),
        pltpu.VMEM((2, S, LANES), jnp.float32),
        pltpu.VMEM((R, LANES), jnp.float32),
        pltpu.VMEM((R, LANES), jnp.float32),
    ]
    kern = functools.partial(_trunk_kernel, TT=TT, NT=NT, NG=NG, has_state=has_state, pos0=pos0, n_kout=n_kout,
                             n_parts=N_PARTS if TT % (N_PARTS * 2) == 0 else 1)
    return pl.pallas_call(
        kern,
        grid=(NG, NT),
        in_specs=in_specs,
        out_specs=out_specs,
        out_shape=out_shape,
        scratch_shapes=scratch,
        compiler_params=pltpu.CompilerParams(
            dimension_semantics=("arbitrary", "arbitrary"),
            vmem_limit_bytes=58 * 1024 * 1024),
        name=name,
    )(*args)


def _prep_weights(norm_pre_a, w_in_a, conv_w_a, conv_b_a, w_gate_r, b_gate_r, w_gate_i, b_gate_i,
                  lru_lambda, w_out_a, norm_post_a, norm_kv, w_kv, norm_pre_b, w_in_b, attn_sinks,
                  w_out_b, norm_post_b):
    bf16 = jnp.bfloat16
    f32 = jnp.float32
    lane_pad = lambda w: jnp.pad(w, ((0, 0), (0, W_PAD)))

    w_g = jnp.stack([w_gate_r[0], w_gate_i[0]]).astype(bf16)
    dense = jnp.concatenate(
        [jnp.pad(w_g[:, n], ((0, 0), (0, 0), (RG_BLOCK * n, D_RNN - RG_BLOCK * (n + 1))))
         for n in range(N_RG_BLOCKS)], axis=1)
    tiles = []
    for k0, c0, c1 in GATE_WINDOWS:
        t = jnp.pad(dense[:, k0:k0 + GATE_K, c0:c1], ((0, 0), (0, 0), (0, MXU_DIM - (c1 - c0))))
        tiles.append(jnp.concatenate([t[0], t[1]], axis=1))
    w_ri = jnp.stack(tiles)

    wk = w_kv[:, :KV_W].astype(bf16).reshape(D_MODEL, N_KV_HEADS, 2, HALF)
    w_kv_p = jnp.concatenate([wk.transpose(0, 2, 1, 3).reshape(D_MODEL, KV_W), w_kv[:, KV_W:].astype(bf16)], axis=1)
    wb = w_in_b[0].astype(bf16)
    wq = wb[:, :Q_W].reshape(D_MODEL, N_KV_HEADS, GROUP, 2, HALF).transpose(0, 3, 2, 1, 4).reshape(D_MODEL, Q_W)
    wg = wb[:, Q_W:].reshape(D_MODEL, N_KV_HEADS, GROUP, HEAD_DIM).transpose(0, 2, 1, 3).reshape(D_MODEL, Q_W)
    w_in_b_p = jnp.concatenate([wq, wg], axis=1)
    w_out_b_p = w_out_b[0].astype(bf16).reshape(N_KV_HEADS, GROUP, HEAD_DIM, D_MODEL).transpose(1, 0, 2, 3).reshape(
        Q_W, D_MODEL)

    inv = ROPE_THETA ** (-jnp.arange(HALF, dtype=f32) / HALF)
    inv_row = jnp.pad(jnp.tile(inv, LANES // HALF), (0, D_MODEL - LANES))
    zeros_m = jnp.zeros((D_MODEL,), f32)
    p_model = jnp.stack([norm_pre_a[0], norm_post_a[0], norm_kv, norm_pre_b[0], norm_post_b[0], inv_row,
                         zeros_m, zeros_m]).astype(f32)
    p_rnn = jnp.concatenate([conv_w_a[0], conv_b_a, b_gate_r, b_gate_i, lru_lambda], axis=0).astype(f32)

    return [
        ("p_model", p_model),
        ("p_rnn", p_rnn),
        ("w_in_a", w_in_a[0].astype(bf16)),
        ("w_ri", w_ri),
        ("w_out_a", lane_pad(w_out_a[0].astype(bf16))),
        ("w_kv", w_kv_p),
        ("w_in_b", lane_pad(w_in_b_p)),
        ("sinks", attn_sinks[0].astype(f32)),
        ("w_out_b", lane_pad(w_out_b_p)),
    ]


def _k_to_cache_layout(k):
    B, R, _ = k.shape
    ka = k[..., :LANES].reshape(B, R, N_KV_HEADS, HALF)
    kb = k[..., LANES:].reshape(B, R, N_KV_HEADS, HALF)
    return jnp.concatenate([ka, kb], axis=-1)


def _k_from_cache_layout(k):
    B, R = k.shape[:2]
    return jnp.concatenate([k[..., :HALF].reshape(B, R, LANES), k[..., HALF:].reshape(B, R, LANES)], axis=-1)


def _to_streams(a, frames):
    NG, n, _, C = a.shape
    return a.reshape(NG, n, frames, NB, C).transpose(0, 3, 1, 2, 4).reshape(NG * NB, n * frames, C)


def kernel(x_prompt, x_sample, state_conv, state_rnn, cache_k, cache_v, norm_pre_a, w_in_a, conv_w_a, conv_b_a, w_gate_r, b_gate_r, w_gate_i, b_gate_i, lru_lambda, w_out_a, norm_post_a, norm_kv, w_kv, norm_pre_b, w_in_b, attn_sinks, w_out_b, norm_post_b):
    weights = _prep_weights(norm_pre_a, w_in_a, conv_w_a, conv_b_a, w_gate_r, b_gate_r, w_gate_i, b_gate_i,
                            lru_lambda, w_out_a, norm_post_a, norm_kv, w_kv, norm_pre_b, w_in_b,
                            attn_sinks, w_out_b, norm_post_b)
    tail = CONV_W - 1

    def unpack(y, t_out, h_out, k_o, v_o, frames):
        NG = t_out.shape[0]
        conv = t_out.reshape(NG, tail, NB, D_RNN).transpose(0, 2, 1, 3).reshape(NG * NB, tail, D_RNN)
        k = _k_to_cache_layout(_to_streams(k_o, frames))
        v = _to_streams(v_o, frames)
        return (y, conv[None], h_out.reshape(NG * NB, D_RNN)[None], k,
                v.reshape(v.shape[0], v.shape[1], N_KV_HEADS, HEAD_DIM))

    prompt = _run_trunk(x_prompt, None, weights, TT=CHUNK, pos0=0, n_kout=WINDOW // CHUNK, name="trunk_prompt")
    yp, pc, pr, pk, pv = unpack(*prompt, CHUNK)

    Bs, Ts, _ = x_sample.shape
    NGs = Bs // NB
    conv0 = state_conv[0].reshape(NGs, NB, tail, D_RNN).transpose(0, 2, 1, 3).reshape(NGs, TAIL, D_RNN)
    h0 = state_rnn[0].reshape(NGs, NB, D_RNN)
    kh0 = _k_from_cache_layout(cache_k)
    vh0 = cache_v.reshape(Bs, WINDOW, KV_W)
    sample = _run_trunk(x_sample, (conv0, h0, kh0, vh0), weights, TT=Ts, pos0=PAST_LEN, n_kout=1,
                        name="trunk_sample")
    ys, sc, sr, sk, sv = unpack(*sample, Ts)

    return (yp, ys, pc, pr, pk, pv, sc, sr, sk, sv)
```

```python
import functools

import numpy as np
import jax
import jax.numpy as jnp
from jax import lax
from jax.experimental import pallas as pl
from jax.experimental.pallas import tpu as pltpu

D_MODEL = 1024
D_RNN = 1408
N_RG_BLOCKS = 16
RG_BLOCK = D_RNN // N_RG_BLOCKS
CONV_W = 4
LRU_C = 8.0
N_HEADS = 16
N_KV_HEADS = 4
HEAD_DIM = 64
HALF = HEAD_DIM // 2
GROUP = N_HEADS // N_KV_HEADS
WINDOW = 128
CHUNK = 64
PAST_LEN = 1024
ROPE_THETA = 10000.0
EPS = 1e-6
LOG2E = 1.4426950408889634

KV_W = N_KV_HEADS * HEAD_DIM
Q_W = N_HEADS * HEAD_DIM

LANES = 128
SUBLANES = 8
MXU_DIM = 256

NB = SUBLANES
TAIL = (CONV_W - 1) * NB

GATE_K = 512
N_GATE_TILES = -(-D_RNN // MXU_DIM)

KEY_WIN = 256
N_PARTS = 1
W_PAD = LANES

P_PRE_A, P_POST_A, P_KV, P_PRE_B, P_POST_B, P_INV_FREQ = range(6)
P_CONV_B, P_B_R, P_B_I, P_LAMBDA = range(CONV_W, CONV_W + 4)


def _gate_windows():
    wins = []
    for j in range(N_GATE_TILES):
        c0 = j * MXU_DIM
        c1 = min(c0 + MXU_DIM, D_RNN)
        lo = (c0 // RG_BLOCK) * RG_BLOCK
        hi = ((c1 - 1) // RG_BLOCK + 1) * RG_BLOCK
        k0 = min((lo // LANES) * LANES, D_RNN - GATE_K)
        assert k0 <= lo and hi <= k0 + GATE_K
        wins.append((k0, c0, c1))
    return wins


GATE_WINDOWS = _gate_windows()


def _bdot(a, b):
    return jnp.dot(a, b, preferred_element_type=jnp.float32)


def _sigmoid(x):
    return 1.0 / (1.0 + jnp.exp(-x))


def _trunk_kernel(*refs, TT, NT, NG, has_state, pos0, n_kout, n_parts):
    it = iter(refs)
    x_hbm = next(it)
    if has_state:
        conv0_ref, h0_ref, kh0_ref, vh0_ref = next(it), next(it), next(it), next(it)
    p_model, p_rnn, w_in_a, w_ri, w_out_a, w_kv, w_in_b, sinks, w_out_b = [next(it) for _ in range(9)]
    y_hbm, tail_out, h_out, k_out, v_out = [next(it) for _ in range(5)]
    (in_buf, out_buf, sem_in, sem_out, sem_kv, ext_ref, tail_ref, hcar_ref, kbuf, vbuf,
     q_il, k_il, v_il, o_il, rope_small, cos_tab, sin_tab) = [next(it) for _ in range(17)]

    g_idx = pl.program_id(0)
    t_idx = pl.program_id(1)
    step = g_idx * NT + t_idx
    n_steps = NT * NG
    slot = step & 1
    R = TT * NB
    f32 = jnp.float32
    bf16 = jnp.bfloat16

    def in_copies(t, g, s):
        return [pltpu.make_async_copy(x_hbm.at[g * NB + b, pl.ds(t * TT, TT), :],
                                      in_buf.at[s, :, b, :], sem_in.at[s]) for b in range(NB)]

    def out_copies(t, g, s):
        return [pltpu.make_async_copy(out_buf.at[s, :, b, :],
                                      y_hbm.at[g * NB + b, pl.ds(t * TT, TT), :], sem_out.at[s]) for b in range(NB)]

    @pl.when(step == 0)
    def _():
        for c in in_copies(t_idx, g_idx, slot):
            c.start()

    @pl.when(step + 1 < n_steps)
    def _():
        nxt = step + 1
        for c in in_copies(nxt % NT, nxt // NT, 1 - slot):
            c.start()

    for c in in_copies(t_idx, g_idx, slot):
        c.wait()

    @pl.when(step >= 2)
    def _():
        for c in out_copies(t_idx, g_idx, slot):
            c.wait()

    if has_state:
        @pl.when(t_idx == 0)
        def _():
            ext_ref[0:TAIL, :] = conv0_ref[...]
            hcar_ref[...] = h0_ref[...]
            zpad = jnp.zeros((NB, KEY_WIN - WINDOW, KV_W), bf16)
            kbuf[:, 0:WINDOW, :] = kh0_ref[...].astype(bf16)
            vbuf[:, 0:WINDOW, :] = vh0_ref[...].astype(bf16)
            kbuf[:, WINDOW:, :] = zpad
            vbuf[:, WINDOW:, :] = zpad
    else:
        @pl.when(t_idx == 0)
        def _():
            ext_ref[0:TAIL, :] = jnp.zeros((TAIL, D_RNN), f32)
            hcar_ref[...] = jnp.zeros((NB, D_RNN), f32)
            kbuf[...] = jnp.zeros(kbuf.shape, bf16)
            vbuf[...] = jnp.zeros(vbuf.shape, bf16)

    @pl.when(t_idx > 0)
    def _():
        ext_ref[0:TAIL, :] = tail_ref[...]

    f0 = pl.multiple_of(t_idx * TT, TT)

    @pl.when(g_idx == 0)
    def _():
        pos = (lax.broadcasted_iota(jnp.int32, (TT, LANES), 0) + (pos0 + t_idx * TT)).astype(f32)
        ang = pos * p_model[P_INV_FREQ:P_INV_FREQ + 1, :LANES]
        rope_small[0, pl.ds(f0, TT), :] = jnp.cos(ang)
        rope_small[1, pl.ds(f0, TT), :] = jnp.sin(ang)

    for t in range(TT):
        cos_tab[t * NB:(t + 1) * NB, :] = jnp.broadcast_to(rope_small[0, pl.ds(f0 + t, 1), :], (NB, LANES))
        sin_tab[t * NB:(t + 1) * NB, :] = jnp.broadcast_to(rope_small[1, pl.ds(f0 + t, 1), :], (NB, LANES))

    def rms_scale(v):
        return lax.rsqrt(jnp.sum(v * v, axis=-1, keepdims=True) * (1.0 / D_MODEL) + EPS)

    FP = TT // n_parts
    RP = FP * NB
    scale = HEAD_DIM ** -0.5 * LOG2E
    z = -p_rnn[P_LAMBDA:P_LAMBDA + 1, :]
    softplus = jnp.maximum(z, 0.0) + jnp.log1p(jnp.exp(-jnp.abs(z)))
    h = hcar_ref[...]
    x1_parts, gate2_parts = [], []
    for part in range(n_parts):
        r0 = part * RP

        x = in_buf[slot, part * FP:(part + 1) * FP].reshape(RP, D_MODEL)
        xn = (x * rms_scale(x) * p_model[P_PRE_A:P_PRE_A + 1, :]).astype(bf16)
        u = _bdot(xn, w_in_a[...])
        branch = u[:, :D_RNN]
        gate = u[:, D_RNN:]

        ext_ref[TAIL + r0:TAIL + r0 + RP, :] = branch
        conv = p_rnn[P_CONV_B:P_CONV_B + 1, :] + branch * p_rnn[CONV_W - 1:CONV_W, :]
        for tap in range(CONV_W - 1):
            conv = conv + ext_ref[tap * NB + r0:tap * NB + r0 + RP, :] * p_rnn[tap:tap + 1, :]
        if part == n_parts - 1:
            new_tail = branch[RP - TAIL:, :]
            tail_ref[...] = new_tail
            tail_out[...] = new_tail

        conv_bf = conv.astype(bf16)
        r_parts, i_parts = [], []
        for j, (k0, c0, c1) in enumerate(GATE_WINDOWS):
            rg = _bdot(conv_bf[:, k0:k0 + GATE_K], w_ri[j])
            r_parts.append(rg[:, :c1 - c0])
            i_parts.append(rg[:, MXU_DIM:MXU_DIM + c1 - c0])
        r_gate = _sigmoid(jnp.concatenate(r_parts, axis=1) + p_rnn[P_B_R:P_B_R + 1, :])
        i_gate = _sigmoid(jnp.concatenate(i_parts, axis=1) + p_rnn[P_B_I:P_B_I + 1, :])

        a = jnp.exp2((-LRU_C * LOG2E * softplus) * r_gate)
        one_m_a2 = 1.0 - a * a
        root = jnp.where(one_m_a2 > 0.0, one_m_a2 * lax.rsqrt(one_m_a2), 0.0)
        bvec = root * (i_gate * conv)

        hs = []
        for t in range(FP):
            h = a[t * NB:(t + 1) * NB, :] * h + bvec[t * NB:(t + 1) * NB, :]
            hs.append(h)
        h_all = jnp.concatenate(hs, axis=0)

        ya = _bdot((h_all * (gate * _sigmoid(gate))).astype(bf16), w_out_a[:, :D_MODEL])
        x1 = x + ya * rms_scale(ya) * p_model[P_POST_A:P_POST_A + 1, :]
        x1_parts.append(x1)

        x1s = x1 * rms_scale(x1)
        kv = _bdot((x1s * p_model[P_KV:P_KV + 1, :]).astype(bf16), w_kv[...])
        ub = _bdot((x1s * p_model[P_PRE_B:P_PRE_B + 1, :]).astype(bf16), w_in_b[:, :2 * Q_W])

        cos = cos_tab[r0:r0 + RP, :]
        sin = sin_tab[r0:r0 + RP, :]
        ka, kb, v = kv[:, :LANES], kv[:, LANES:2 * LANES], kv[:, 2 * LANES:]
        k_il[0, r0:r0 + RP, :] = ka * cos - kb * sin
        k_il[1, r0:r0 + RP, :] = kb * cos + ka * sin
        v_il[0, r0:r0 + RP, :] = v[:, :LANES]
        v_il[1, r0:r0 + RP, :] = v[:, LANES:]
        for j in range(GROUP):
            qa = ub[:, j * LANES:(j + 1) * LANES]
            qb = ub[:, Q_W // 2 + j * LANES:Q_W // 2 + (j + 1) * LANES]
            q_il[j, r0:r0 + RP, :] = (qa * cos - qb * sin) * scale
            q_il[GROUP + j, r0:r0 + RP, :] = (qb * cos + qa * sin) * scale
        gate2_parts.append(ub[:, Q_W:])
    hcar_ref[...] = h
    h_out[...] = h

    if has_state:
        cur_off = WINDOW
        n_valid = WINDOW + TT
    else:
        ring = WINDOW // TT + 1
        cur_off = pl.multiple_of((t_idx % ring) * TT, TT)
        n_valid = jnp.minimum(t_idx + 1, ring) * TT

    lane_q = lax.broadcasted_iota(jnp.int32, (TT, 2 * LANES), 1)
    q_slot = (lane_q & (LANES - 1)) // HALF
    first_half = lax.broadcasted_iota(jnp.int32, (TT, LANES), 1) < HEAD_DIM
    invalid = lax.broadcasted_iota(jnp.int32, (TT, KEY_WIN), 1) >= n_valid
    col_row = lax.broadcasted_iota(jnp.int32, (1, KEY_WIN), 1)

    def attend_stream(b, carry):
        rows = pl.ds(b, TT, stride=NB)
        kbuf[b, pl.ds(cur_off, TT), :] = jnp.concatenate(
            [k_il[0, rows, :], k_il[1, rows, :]], axis=1).astype(bf16)
        vbuf[b, pl.ds(cur_off, TT), :] = jnp.concatenate(
            [v_il[0, rows, :], v_il[1, rows, :]], axis=1).astype(bf16)
        qcat = [jnp.concatenate([q_il[j, rows, :], q_il[GROUP + j, rows, :]], axis=1) for j in range(GROUP)]
        qstack = jnp.concatenate(
            [jnp.where(q_slot == h // GROUP, qcat[h % GROUP], 0.0) for h in range(N_HEADS)],
            axis=0).astype(bf16)
        s_all = lax.dot_general(qstack, kbuf[b], (((1,), (1,)), ((), ())),
                                preferred_element_type=f32)
        p_parts, inv_l = [], []
        for h in range(N_HEADS):
            fill = jnp.where(col_row == n_valid, sinks[h] * LOG2E, -jnp.inf)
            sh = jnp.where(invalid, fill, s_all[h * TT:(h + 1) * TT, :])
            ph = jnp.exp2(sh - jnp.max(sh, axis=-1, keepdims=True))
            inv_l.append(1.0 / jnp.sum(ph, axis=-1, keepdims=True))
            p_parts.append(ph.astype(bf16))
        o_g = [_bdot(jnp.concatenate(p_parts[g * GROUP:(g + 1) * GROUP], axis=0),
                     vbuf[b, :, (g // 2) * LANES:(g // 2 + 1) * LANES]) for g in range(N_KV_HEADS)]
        for j in range(GROUP):
            blk = [o_g[g][j * TT:(j + 1) * TT, :] * inv_l[g * GROUP + j] for g in range(N_KV_HEADS)]
            o_il[2 * j, rows, :] = jnp.where(first_half, blk[0], blk[1])
            o_il[2 * j + 1, rows, :] = jnp.where(first_half, blk[2], blk[3])
        return carry

    lax.fori_loop(0, NB, attend_stream, 0, unroll=8)

    for part in range(n_parts):
        r0 = part * RP
        o2 = jnp.concatenate([o_il[l, r0:r0 + RP, :] for l in range(Q_W // LANES)], axis=1)
        gate2 = gate2_parts[part]
        og = (o2 * (gate2 * _sigmoid(gate2))).astype(bf16)
        yb = _bdot(og, w_out_b[:, :D_MODEL])
        y = x1_parts[part] + yb * rms_scale(yb) * p_model[P_POST_B:P_POST_B + 1, :]
        out_buf[slot, part * FP:(part + 1) * FP] = y.reshape(FP, NB, D_MODEL)
    for c in out_copies(t_idx, g_idx, slot):
        c.start()

    @pl.when(t_idx >= NT - n_kout)
    def _():
        tt = t_idx - (NT - n_kout)
        cps = []
        for l in range(KV_W // LANES):
            cps.append(pltpu.make_async_copy(k_il.at[l], k_out.at[g_idx, tt, :, pl.ds(l * LANES, LANES)],
                                             sem_kv.at[0]))
            cps.append(pltpu.make_async_copy(v_il.at[l], v_out.at[g_idx, tt, :, pl.ds(l * LANES, LANES)],
                                             sem_kv.at[1]))
        for c in cps:
            c.start()
        for c in cps:
            c.wait()

    @pl.when(step == n_steps - 1)
    def _():
        if n_steps >= 2:
            for c in out_copies(t_idx, g_idx, 1 - slot):
                c.wait()
        for c in out_copies(t_idx, g_idx, slot):
            c.wait()


def _const_spec(shape):
    nd = len(shape)
    return pl.BlockSpec(shape, lambda g, t, _nd=nd: (0,) * _nd, pipeline_mode=pl.Buffered(1))


def _run_trunk(x, state, weights, *, TT, pos0, n_kout, name):
    B, S, _ = x.shape
    NT = S // TT
    NG = B // NB
    R = TT * NB
    assert NT * TT == S and NG * NB == B and n_kout <= NT
    assert state is not None or WINDOW % TT == 0
    has_state = state is not None

    in_specs = [pl.BlockSpec(memory_space=pl.ANY)]
    args = [x]
    if has_state:
        conv0, h0, kh0, vh0 = state
        in_specs += [
            pl.BlockSpec((None, TAIL, D_RNN), lambda g, t: (g, 0, 0)),
            pl.BlockSpec((None, NB, D_RNN), lambda g, t: (g, 0, 0)),
            pl.BlockSpec((NB, WINDOW, KV_W), lambda g, t: (g, 0, 0)),
            pl.BlockSpec((NB, WINDOW, KV_W), lambda g, t: (g, 0, 0)),
        ]
        args += [conv0, h0, kh0, vh0]
    for name_w, w in weights:
        if name_w == "sinks":
            in_specs.append(pl.BlockSpec(memory_space=pltpu.SMEM))
        else:
            in_specs.append(_const_spec(w.shape))
        args.append(w)

    out_shape = (
        jax.ShapeDtypeStruct((B, S, D_MODEL), jnp.float32),
        jax.ShapeDtypeStruct((NG, TAIL, D_RNN), jnp.float32),
        jax.ShapeDtypeStruct((NG, NB, D_RNN), jnp.float32),
        jax.ShapeDtypeStruct((NG, n_kout, R, KV_W), jnp.float32),
        jax.ShapeDtypeStruct((NG, n_kout, R, KV_W), jnp.float32),
    )
    out_specs = (
        pl.BlockSpec(memory_space=pl.ANY),
        pl.BlockSpec((None, TAIL, D_RNN), lambda g, t: (g, 0, 0)),
        pl.BlockSpec((None, NB, D_RNN), lambda g, t: (g, 0, 0)),
        pl.BlockSpec(memory_space=pl.ANY),
        pl.BlockSpec(memory_space=pl.ANY),
    )
    scratch = [
        pltpu.VMEM((2, TT, NB, D_MODEL), jnp.float32),
        pltpu.VMEM((2, TT, NB, D_MODEL), jnp.float32),
        pltpu.SemaphoreType.DMA((2,)),
        pltpu.SemaphoreType.DMA((2,)),
        pltpu.SemaphoreType.DMA((2,)),
        pltpu.VMEM((TAIL + R, D_RNN), jnp.float32),
        pltpu.VMEM((TAIL, D_RNN), jnp.float32),
        pltpu.VMEM((NB, D_RNN), jnp.float32),
        pltpu.VMEM((NB, KEY_WIN, KV_W), jnp.bfloat16),
        pltpu.VMEM((NB, KEY_WIN, KV_W), jnp.bfloat16),
        pltpu.VMEM((Q_W // LANES, R, LANES), jnp.float32),
        pltpu.VMEM((KV_W // LANES, R, LANES), jnp.float32),
        pltpu.VMEM((KV_W // LANES, R, LANES), jnp.float32),
        pltpu.VMEM((Q_W // LANES, R, LANES), jnp.float32),
        pltpu.VMEM((2, S, LANES), jnp.float32),
        pltpu.VMEM((R, LANES), jnp.float32),
        pltpu.VMEM((R, LANES), jnp.float32),
    ]
    kern = functools.partial(_trunk_kernel, TT=TT, NT=NT, NG=NG, has_state=has_state, pos0=pos0, n_kout=n_kout,
                             n_parts=N_PARTS if TT % (N_PARTS * 2) == 0 else 1)
    return pl.pallas_call(
        kern,
        grid=(NG, NT),
        in_specs=in_specs,
        out_specs=out_specs,
        out_shape=out_shape,
        scratch_shapes=scratch,
        compiler_params=pltpu.CompilerParams(
            dimension_semantics=("arbitrary", "arbitrary"),
            vmem_limit_bytes=58 * 1024 * 1024),
        name=name,
    )(*args)


def _prep_weights(norm_pre_a, w_in_a, conv_w_a, conv_b_a, w_gate_r, b_gate_r, w_gate_i, b_gate_i,
                  lru_lambda, w_out_a, norm_post_a, norm_kv, w_kv, norm_pre_b, w_in_b, attn_sinks,
                  w_out_b, norm_post_b):
    bf16 = jnp.bfloat16
    f32 = jnp.float32
    lane_pad = lambda w: jnp.pad(w, ((0, 0), (0, W_PAD)))

    w_g = jnp.stack([w_gate_r[0], w_gate_i[0]]).astype(bf16)
    dense = jnp.concatenate(
        [jnp.pad(w_g[:, n], ((0, 0), (0, 0), (RG_BLOCK * n, D_RNN - RG_BLOCK * (n + 1))))
         for n in range(N_RG_BLOCKS)], axis=1)
    tiles = []
    for k0, c0, c1 in GATE_WINDOWS:
        t = jnp.pad(dense[:, k0:k0 + GATE_K, c0:c1], ((0, 0), (0, 0), (0, MXU_DIM - (c1 - c0))))
        tiles.append(jnp.concatenate([t[0], t[1]], axis=1))
    w_ri = jnp.stack(tiles)

    wk = w_kv[:, :KV_W].astype(bf16).reshape(D_MODEL, N_KV_HEADS, 2, HALF)
    w_kv_p = jnp.concatenate([wk.transpose(0, 2, 1, 3).reshape(D_MODEL, KV_W), w_kv[:, KV_W:].astype(bf16)], axis=1)
    wb = w_in_b[0].astype(bf16)
    wq = wb[:, :Q_W].reshape(D_MODEL, N_KV_HEADS, GROUP, 2, HALF).transpose(0, 3, 2, 1, 4).reshape(D_MODEL, Q_W)
    wg = wb[:, Q_W:].reshape(D_MODEL, N_KV_HEADS, GROUP, HEAD_DIM).transpose(0, 2, 1, 3).reshape(D_MODEL, Q_W)
    w_in_b_p = jnp.concatenate([wq, wg], axis=1)
    w_out_b_p = w_out_b[0].astype(bf16).reshape(N_KV_HEADS, GROUP, HEAD_DIM, D_MODEL).transpose(1, 0, 2, 3).reshape(
        Q_W, D_MODEL)

    inv = ROPE_THETA ** (-jnp.arange(HALF, dtype=f32) / HALF)
    inv_row = jnp.pad(jnp.tile(inv, LANES // HALF), (0, D_MODEL - LANES))
    zeros_m = jnp.zeros((D_MODEL,), f32)
    p_model = jnp.stack([norm_pre_a[0], norm_post_a[0], norm_kv, norm_pre_b[0], norm_post_b[0], inv_row,
                         zeros_m, zeros_m]).astype(f32)
    p_rnn = jnp.concatenate([conv_w_a[0], conv_b_a, b_gate_r, b_gate_i, lru_lambda], axis=0).astype(f32)

    return [
        ("p_model", p_model),
        ("p_rnn", p_rnn),
        ("w_in_a", w_in_a[0].astype(bf16)),
        ("w_ri", w_ri),
        ("w_out_a", lane_pad(w_out_a[0].astype(bf16))),
        ("w_kv", w_kv_p),
        ("w_in_b", lane_pad(w_in_b_p)),
        ("sinks", attn_sinks[0].astype(f32)),
        ("w_out_b", lane_pad(w_out_b_p)),
    ]


def _k_to_cache_layout(k):
    B, R, _ = k.shape
    ka = k[..., :LANES].reshape(B, R, N_KV_HEADS, HALF)
    kb = k[..., LANES:].reshape(B, R, N_KV_HEADS, HALF)
    return jnp.concatenate([ka, kb], axis=-1)


def _k_from_cache_layout(k):
    B, R = k.shape[:2]
    return jnp.concatenate([k[..., :HALF].reshape(B, R, LANES), k[..., HALF:].reshape(B, R, LANES)], axis=-1)


def _to_streams(a, frames):
    NG, n, _, C = a.shape
    return a.reshape(NG, n, frames, NB, C).transpose(0, 3, 1, 2, 4).reshape(NG * NB, n * frames, C)


def kernel(x_prompt, x_sample, state_conv, state_rnn, cache_k, cache_v, norm_pre_a, w_in_a, conv_w_a, conv_b_a, w_gate_r, b_gate_r, w_gate_i, b_gate_i, lru_lambda, w_out_a, norm_post_a, norm_kv, w_kv, norm_pre_b, w_in_b, attn_sinks, w_out_b, norm_post_b):
    weights = _prep_weights(norm_pre_a, w_in_a, conv_w_a, conv_b_a, w_gate_r, b_gate_r, w_gate_i, b_gate_i,
                            lru_lambda, w_out_a, norm_post_a, norm_kv, w_kv, norm_pre_b, w_in_b,
                            attn_sinks, w_out_b, norm_post_b)
    tail = CONV_W - 1

    def unpack(y, t_out, h_out, k_o, v_o, frames):
        NG = t_out.shape[0]
        conv = t_out.reshape(NG, tail, NB, D_RNN).transpose(0, 2, 1, 3).reshape(NG * NB, tail, D_RNN)
        k = _k_to_cache_layout(_to_streams(k_o, frames))
        v = _to_streams(v_o, frames)
        return (y, conv[None], h_out.reshape(NG * NB, D_RNN)[None], k,
                v.reshape(v.shape[0], v.shape[1], N_KV_HEADS, HEAD_DIM))

    prompt = _run_trunk(x_prompt, None, weights, TT=CHUNK, pos0=0, n_kout=WINDOW // CHUNK, name="trunk_prompt")
    yp, pc, pr, pk, pv = unpack(*prompt, CHUNK)

    Bs, Ts, _ = x_sample.shape
    NGs = Bs // NB
    conv0 = state_conv[0].reshape(NGs, NB, tail, D_RNN).transpose(0, 2, 1, 3).reshape(NGs, TAIL, D_RNN)
    h0 = state_rnn[0].reshape(NGs, NB, D_RNN)
    kh0 = _k_from_cache_layout(cache_k)
    vh0 = cache_v.reshape(Bs, WINDOW, KV_W)
    sample = _run_trunk(x_sample, (conv0, h0, kh0, vh0), weights, TT=Ts, pos0=PAST_LEN, n_kout=1,
                        name="trunk_sample")
    ys, sc, sr, sk, sv = unpack(*sample, Ts)

    return (yp, ys, pc, pr, pk, pv, sc, sr, sk, sv)
```

```python
import functools

import numpy as np
import jax
import jax.numpy as jnp
from jax import lax
from jax.experimental import pallas as pl
from jax.experimental.pallas import tpu as pltpu

D_MODEL = 1024
D_RNN = 1408
N_RG_BLOCKS = 16
RG_BLOCK = D_RNN // N_RG_BLOCKS
CONV_W = 4
LRU_C = 8.0
N_HEADS = 16
N_KV_HEADS = 4
HEAD_DIM = 64
HALF = HEAD_DIM // 2
GROUP = N_HEADS // N_KV_HEADS
WINDOW = 128
CHUNK = 64
PAST_LEN = 1024
ROPE_THETA = 10000.0
EPS = 1e-6
LOG2E = 1.4426950408889634

KV_W = N_KV_HEADS * HEAD_DIM
Q_W = N_HEADS * HEAD_DIM

LANES = 128
SUBLANES = 8
MXU_DIM = 256

NB = SUBLANES
TAIL = (CONV_W - 1) * NB

GATE_K = 512
N_GATE_TILES = -(-D_RNN // MXU_DIM)

KEY_WIN = 256
N_PARTS = 1
W_PAD = LANES

P_PRE_A, P_POST_A, P_KV, P_PRE_B, P_POST_B, P_INV_FREQ = range(6)
P_CONV_B, P_B_R, P_B_I, P_LAMBDA = range(CONV_W, CONV_W + 4)


def _gate_windows():
    wins = []
    for j in range(N_GATE_TILES):
        c0 = j * MXU_DIM
        c1 = min(c0 + MXU_DIM, D_RNN)
        lo = (c0 // RG_BLOCK) * RG_BLOCK
        hi = ((c1 - 1) // RG_BLOCK + 1) * RG_BLOCK
        k0 = min((lo // LANES) * LANES, D_RNN - GATE_K)
        assert k0 <= lo and hi <= k0 + GATE_K
        wins.append((k0, c0, c1))
    return wins


GATE_WINDOWS = _gate_windows()


def _bdot(a, b):
    return jnp.dot(a, b, preferred_element_type=jnp.float32)


def _sigmoid(x):
    return 1.0 / (1.0 + jnp.exp(-x))


def _trunk_kernel(*refs, TT, NT, NG, has_state, pos0, n_kout, n_parts):
    it = iter(refs)
    x_hbm = next(it)
    if has_state:
        conv0_ref, h0_ref, kh0_ref, vh0_ref = next(it), next(it), next(it), next(it)
    p_model, p_rnn, w_in_a, w_ri, w_out_a, w_kv, w_in_b, sinks, w_out_b = [next(it) for _ in range(9)]
    y_hbm, tail_out, h_out, k_out, v_out = [next(it) for _ in range(5)]
    (in_buf, out_buf, sem_in, sem_out, sem_kv, ext_ref, tail_ref, hcar_ref, kbuf, vbuf,
     q_il, k_il, v_il, o_il, rope_small, cos_tab, sin_tab, xn_scr) = [next(it) for _ in range(18)]

    g_idx = pl.program_id(0)
    t_idx = pl.program_id(1)
    step = g_idx * NT + t_idx
    n_steps = NT * NG
    slot = step & 1
    R = TT * NB
    f32 = jnp.float32
    bf16 = jnp.bfloat16

    def in_copies(t, g, s):
        return [pltpu.make_async_copy(x_hbm.at[g * NB + b, pl.ds(t * TT, TT), :],
                                      in_buf.at[s, :, b, :], sem_in.at[s]) for b in range(NB)]

    def out_copies(t, g):
        return [pltpu.make_async_copy(out_buf.at[:, b, :],
                                      y_hbm.at[g * NB + b, pl.ds(t * TT, TT), :], sem_out.at[0]) for b in range(NB)]

    def rms_scale(v):
        return lax.rsqrt(jnp.sum(v * v, axis=-1, keepdims=True) * (1.0 / D_MODEL) + EPS)

    def normed_input(s):
        xs = in_buf[s].reshape(R, D_MODEL)
        return (xs * rms_scale(xs) * p_model[P_PRE_A:P_PRE_A + 1, :]).astype(bf16)

    @pl.when(step == 0)
    def _():
        for c in in_copies(t_idx, g_idx, slot):
            c.start()
        for c in in_copies(t_idx, g_idx, slot):
            c.wait()
        xn_scr[...] = normed_input(slot)

    @pl.when(step + 1 < n_steps)
    def _():
        nxt = step + 1
        for c in in_copies(nxt % NT, nxt // NT, 1 - slot):
            c.start()

    if has_state:
        @pl.when(t_idx == 0)
        def _():
            ext_ref[0:TAIL, :] = conv0_ref[...]
            hcar_ref[...] = h0_ref[...]
            zpad = jnp.zeros((NB, KEY_WIN - WINDOW, KV_W), bf16)
            kbuf[:, 0:WINDOW, :] = kh0_ref[...].astype(bf16)
            vbuf[:, 0:WINDOW, :] = vh0_ref[...].astype(bf16)
            kbuf[:, WINDOW:, :] = zpad
            vbuf[:, WINDOW:, :] = zpad
    else:
        @pl.when(t_idx == 0)
        def _():
            ext_ref[0:TAIL, :] = jnp.zeros((TAIL, D_RNN), f32)
            hcar_ref[...] = jnp.zeros((NB, D_RNN), f32)
            kbuf[...] = jnp.zeros(kbuf.shape, bf16)
            vbuf[...] = jnp.zeros(vbuf.shape, bf16)

    @pl.when(t_idx > 0)
    def _():
        ext_ref[0:TAIL, :] = tail_ref[...]

    f0 = pl.multiple_of(t_idx * TT, TT)

    @pl.when(g_idx == 0)
    def _():
        pos = (lax.broadcasted_iota(jnp.int32, (TT, LANES), 0) + (pos0 + t_idx * TT)).astype(f32)
        ang = pos * p_model[P_INV_FREQ:P_INV_FREQ + 1, :LANES]
        rope_small[0, pl.ds(f0, TT), :] = jnp.cos(ang)
        rope_small[1, pl.ds(f0, TT), :] = jnp.sin(ang)

    for t in range(TT):
        cos_tab[t * NB:(t + 1) * NB, :] = jnp.broadcast_to(rope_small[0, pl.ds(f0 + t, 1), :], (NB, LANES))
        sin_tab[t * NB:(t + 1) * NB, :] = jnp.broadcast_to(rope_small[1, pl.ds(f0 + t, 1), :], (NB, LANES))

    FP = TT // n_parts
    RP = FP * NB
    scale = HEAD_DIM ** -0.5 * LOG2E
    z = -p_rnn[P_LAMBDA:P_LAMBDA + 1, :]
    softplus = jnp.maximum(z, 0.0) + jnp.log1p(jnp.exp(-jnp.abs(z)))
    h = hcar_ref[...]
    x1_parts, gate2_parts = [], []
    for part in range(n_parts):
        r0 = part * RP

        x = in_buf[slot, part * FP:(part + 1) * FP].reshape(RP, D_MODEL)
        u = _bdot(xn_scr[r0:r0 + RP, :], w_in_a[...])
        branch = u[:, :D_RNN]
        gate = u[:, D_RNN:]

        ext_ref[TAIL + r0:TAIL + r0 + RP, :] = branch
        conv = p_rnn[P_CONV_B:P_CONV_B + 1, :] + branch * p_rnn[CONV_W - 1:CONV_W, :]
        for tap in range(CONV_W - 1):
            conv = conv + ext_ref[tap * NB + r0:tap * NB + r0 + RP, :] * p_rnn[tap:tap + 1, :]
        if part == n_parts - 1:
            new_tail = branch[RP - TAIL:, :]
            tail_ref[...] = new_tail
            tail_out[...] = new_tail

        conv_bf = conv.astype(bf16)
        r_parts, i_parts = [], []
        for j, (k0, c0, c1) in enumerate(GATE_WINDOWS):
            rg = _bdot(conv_bf[:, k0:k0 + GATE_K], w_ri[j])
            r_parts.append(rg[:, :c1 - c0])
            i_parts.append(rg[:, MXU_DIM:MXU_DIM + c1 - c0])
        r_gate = _sigmoid(jnp.concatenate(r_parts, axis=1) + p_rnn[P_B_R:P_B_R + 1, :])
        i_gate = _sigmoid(jnp.concatenate(i_parts, axis=1) + p_rnn[P_B_I:P_B_I + 1, :])

        a = jnp.exp2((-LRU_C * LOG2E * softplus) * r_gate)
        one_m_a2 = 1.0 - a * a
        root = jnp.where(one_m_a2 > 0.0, one_m_a2 * lax.rsqrt(one_m_a2), 0.0)
        bvec = root * (i_gate * conv)

        hs = []
        for t in range(FP):
            h = a[t * NB:(t + 1) * NB, :] * h + bvec[t * NB:(t + 1) * NB, :]
            hs.append(h)
        h_all = jnp.concatenate(hs, axis=0)

        ya = _bdot((h_all * (gate * _sigmoid(gate))).astype(bf16), w_out_a[:, :D_MODEL])
        x1 = x + ya * rms_scale(ya) * p_model[P_POST_A:P_POST_A + 1, :]
        x1_parts.append(x1)

        x1s = x1 * rms_scale(x1)
        kv = _bdot((x1s * p_model[P_KV:P_KV + 1, :]).astype(bf16), w_kv[...])
        ub = _bdot((x1s * p_model[P_PRE_B:P_PRE_B + 1, :]).astype(bf16), w_in_b[:, :2 * Q_W])

        cos = cos_tab[r0:r0 + RP, :]
        sin = sin_tab[r0:r0 + RP, :]
        ka, kb, v = kv[:, :LANES], kv[:, LANES:2 * LANES], kv[:, 2 * LANES:]
        k_il[0, r0:r0 + RP, :] = ka * cos - kb * sin
        k_il[1, r0:r0 + RP, :] = kb * cos + ka * sin
        v_il[0, r0:r0 + RP, :] = v[:, :LANES]
        v_il[1, r0:r0 + RP, :] = v[:, LANES:]
        for j in range(GROUP):
            qa = ub[:, j * LANES:(j + 1) * LANES]
            qb = ub[:, Q_W // 2 + j * LANES:Q_W // 2 + (j + 1) * LANES]
            q_il[j, r0:r0 + RP, :] = (qa * cos - qb * sin) * scale
            q_il[GROUP + j, r0:r0 + RP, :] = (qb * cos + qa * sin) * scale
        gate2_parts.append(ub[:, Q_W:])
    hcar_ref[...] = h
    h_out[...] = h

    @pl.when(t_idx >= NT - n_kout)
    def _():
        tt = t_idx - (NT - n_kout)
        cps = []
        for l in range(KV_W // LANES):
            cps.append(pltpu.make_async_copy(k_il.at[l], k_out.at[g_idx, tt, :, pl.ds(l * LANES, LANES)],
                                             sem_kv.at[0]))
            cps.append(pltpu.make_async_copy(v_il.at[l], v_out.at[g_idx, tt, :, pl.ds(l * LANES, LANES)],
                                             sem_kv.at[1]))
        for c in cps:
            c.start()
        for c in cps:
            c.wait()

    if has_state:
        cur_off = WINDOW
        n_valid = WINDOW + TT
    else:
        ring = WINDOW // TT + 1
        cur_off = pl.multiple_of((t_idx % ring) * TT, TT)
        n_valid = jnp.minimum(t_idx + 1, ring) * TT

    lane_q = lax.broadcasted_iota(jnp.int32, (TT, 2 * LANES), 1)
    q_slot = (lane_q & (LANES - 1)) // HALF
    first_half = lax.broadcasted_iota(jnp.int32, (TT, LANES), 1) < HEAD_DIM
    invalid = lax.broadcasted_iota(jnp.int32, (TT, KEY_WIN), 1) >= n_valid
    col_row = lax.broadcasted_iota(jnp.int32, (1, KEY_WIN), 1)

    def attend_stream(b, carry):
        rows = pl.ds(b, TT, stride=NB)
        kbuf[b, pl.ds(cur_off, TT), :] = jnp.concatenate(
            [k_il[0, rows, :], k_il[1, rows, :]], axis=1).astype(bf16)
        vbuf[b, pl.ds(cur_off, TT), :] = jnp.concatenate(
            [v_il[0, rows, :], v_il[1, rows, :]], axis=1).astype(bf16)
        qcat = [jnp.concatenate([q_il[j, rows, :], q_il[GROUP + j, rows, :]], axis=1) for j in range(GROUP)]
        qstack = jnp.concatenate(
            [jnp.where(q_slot == h // GROUP, qcat[h % GROUP], 0.0) for h in range(N_HEADS)],
            axis=0).astype(bf16)
        s_all = lax.dot_general(qstack, kbuf[b], (((1,), (1,)), ((), ())),
                                preferred_element_type=f32)
        p_parts, inv_l = [], []
        for h in range(N_HEADS):
            fill = jnp.where(col_row == n_valid, sinks[h] * LOG2E, -jnp.inf)
            sh = jnp.where(invalid, fill, s_all[h * TT:(h + 1) * TT, :])
            ph = jnp.exp2(sh - jnp.max(sh, axis=-1, keepdims=True))
            inv_l.append(1.0 / jnp.sum(ph, axis=-1, keepdims=True))
            p_parts.append(ph.astype(bf16))
        o_g = [_bdot(jnp.concatenate(p_parts[g * GROUP:(g + 1) * GROUP], axis=0),
                     vbuf[b, :, (g // 2) * LANES:(g // 2 + 1) * LANES]) for g in range(N_KV_HEADS)]
        for j in range(GROUP):
            blk = [o_g[g][j * TT:(j + 1) * TT, :] * inv_l[g * GROUP + j] for g in range(N_KV_HEADS)]
            o_il[2 * j, rows, :] = jnp.where(first_half, blk[0], blk[1])
            o_il[2 * j + 1, rows, :] = jnp.where(first_half, blk[2], blk[3])
        return carry

    lax.fori_loop(0, NB, attend_stream, 0, unroll=8)

    @pl.when(step >= 1)
    def _():
        for c in out_copies(t_idx, g_idx):
            c.wait()

    @pl.when(step + 1 < n_steps)
    def _():
        nxt = step + 1
        for c in in_copies(nxt % NT, nxt // NT, 1 - slot):
            c.wait()

    xn_scr[...] = normed_input(1 - slot)

    for part in range(n_parts):
        r0 = part * RP
        o2 = jnp.concatenate([o_il[l, r0:r0 + RP, :] for l in range(Q_W // LANES)], axis=1)
        gate2 = gate2_parts[part]
        og = (o2 * (gate2 * _sigmoid(gate2))).astype(bf16)
        yb = _bdot(og, w_out_b[:, :D_MODEL])
        y = x1_parts[part] + yb * rms_scale(yb) * p_model[P_POST_B:P_POST_B + 1, :]
        out_buf[part * FP:(part + 1) * FP] = y.reshape(FP, NB, D_MODEL)
    for c in out_copies(t_idx, g_idx):
        c.start()

    @pl.when(step == n_steps - 1)
    def _():
        for c in out_copies(t_idx, g_idx):
            c.wait()


def _const_spec(shape):
    nd = len(shape)
    return pl.BlockSpec(shape, lambda g, t, _nd=nd: (0,) * _nd, pipeline_mode=pl.Buffered(1))


def _run_trunk(x, state, weights, *, TT, pos0, n_kout, name):
    B, S, _ = x.shape
    NT = S // TT
    NG = B // NB
    R = TT * NB
    assert NT * TT == S and NG * NB == B and n_kout <= NT
    assert state is not None or WINDOW % TT == 0
    has_state = state is not None

    in_specs = [pl.BlockSpec(memory_space=pl.ANY)]
    args = [x]
    if has_state:
        conv0, h0, kh0, vh0 = state
        in_specs += [
            pl.BlockSpec((None, TAIL, D_RNN), lambda g, t: (g, 0, 0)),
            pl.BlockSpec((None, NB, D_RNN), lambda g, t: (g, 0, 0)),
            pl.BlockSpec((NB, WINDOW, KV_W), lambda g, t: (g, 0, 0)),
            pl.BlockSpec((NB, WINDOW, KV_W), lambda g, t: (g, 0, 0)),
        ]
        args += [conv0, h0, kh0, vh0]
    for name_w, w in weights:
        if name_w == "sinks":
            in_specs.append(pl.BlockSpec(memory_space=pltpu.SMEM))
        else:
            in_specs.append(_const_spec(w.shape))
        args.append(w)

    out_shape = (
        jax.ShapeDtypeStruct((B, S, D_MODEL), jnp.float32),
        jax.ShapeDtypeStruct((NG, TAIL, D_RNN), jnp.float32),
        jax.ShapeDtypeStruct((NG, NB, D_RNN), jnp.float32),
        jax.ShapeDtypeStruct((NG, n_kout, R, KV_W), jnp.float32),
        jax.ShapeDtypeStruct((NG, n_kout, R, KV_W), jnp.float32),
    )
    out_specs = (
        pl.BlockSpec(memory_space=pl.ANY),
        pl.BlockSpec((None, TAIL, D_RNN), lambda g, t: (g, 0, 0)),
        pl.BlockSpec((None, NB, D_RNN), lambda g, t: (g, 0, 0)),
        pl.BlockSpec(memory_space=pl.ANY),
        pl.BlockSpec(memory_space=pl.ANY),
    )
    scratch = [
        pltpu.VMEM((2, TT, NB, D_MODEL), jnp.float32),
        pltpu.VMEM((TT, NB, D_MODEL), jnp.float32),
        pltpu.SemaphoreType.DMA((2,)),
        pltpu.SemaphoreType.DMA((1,)),
        pltpu.SemaphoreType.DMA((2,)),
        pltpu.VMEM((TAIL + R, D_RNN), jnp.float32),
        pltpu.VMEM((TAIL, D_RNN), jnp.float32),
        pltpu.VMEM((NB, D_RNN), jnp.float32),
        pltpu.VMEM((NB, KEY_WIN, KV_W), jnp.bfloat16),
        pltpu.VMEM((NB, KEY_WIN, KV_W), jnp.bfloat16),
        pltpu.VMEM((Q_W // LANES, R, LANES), jnp.float32),
        pltpu.VMEM((KV_W // LANES, R, LANES), jnp.float32),
        pltpu.VMEM((KV_W // LANES, R, LANES), jnp.float32),
        pltpu.VMEM((Q_W // LANES, R, LANES), jnp.float32),
        pltpu.VMEM((2, S, LANES), jnp.float32),
        pltpu.VMEM((R, LANES), jnp.float32),
        pltpu.VMEM((R, LANES), jnp.float32),
        pltpu.VMEM((R, D_MODEL), jnp.bfloat16),
    ]
    kern = functools.partial(_trunk_kernel, TT=TT, NT=NT, NG=NG, has_state=has_state, pos0=pos0, n_kout=n_kout,
                             n_parts=N_PARTS if TT % (N_PARTS * 2) == 0 else 1)
    return pl.pallas_call(
        kern,
        grid=(NG, NT),
        in_specs=in_specs,
        out_specs=out_specs,
        out_shape=out_shape,
        scratch_shapes=scratch,
        compiler_params=pltpu.CompilerParams(
            dimension_semantics=("arbitrary", "arbitrary"),
            vmem_limit_bytes=58 * 1024 * 1024),
        name=name,
    )(*args)


def _prep_weights(norm_pre_a, w_in_a, conv_w_a, conv_b_a, w_gate_r, b_gate_r, w_gate_i, b_gate_i,
                  lru_lambda, w_out_a, norm_post_a, norm_kv, w_kv, norm_pre_b, w_in_b, attn_sinks,
                  w_out_b, norm_post_b):
    bf16 = jnp.bfloat16
    f32 = jnp.float32
    lane_pad = lambda w: jnp.pad(w, ((0, 0), (0, W_PAD)))

    w_g = jnp.stack([w_gate_r[0], w_gate_i[0]]).astype(bf16)
    dense = jnp.concatenate(
        [jnp.pad(w_g[:, n], ((0, 0), (0, 0), (RG_BLOCK * n, D_RNN - RG_BLOCK * (n + 1))))
         for n in range(N_RG_BLOCKS)], axis=1)
    tiles = []
    for k0, c0, c1 in GATE_WINDOWS:
        t = jnp.pad(dense[:, k0:k0 + GATE_K, c0:c1], ((0, 0), (0, 0), (0, MXU_DIM - (c1 - c0))))
        tiles.append(jnp.concatenate([t[0], t[1]], axis=1))
    w_ri = jnp.stack(tiles)

    wk = w_kv[:, :KV_W].astype(bf16).reshape(D_MODEL, N_KV_HEADS, 2, HALF)
    w_kv_p = jnp.concatenate([wk.transpose(0, 2, 1, 3).reshape(D_MODEL, KV_W), w_kv[:, KV_W:].astype(bf16)], axis=1)
    wb = w_in_b[0].astype(bf16)
    wq = wb[:, :Q_W].reshape(D_MODEL, N_KV_HEADS, GROUP, 2, HALF).transpose(0, 3, 2, 1, 4).reshape(D_MODEL, Q_W)
    wg = wb[:, Q_W:].reshape(D_MODEL, N_KV_HEADS, GROUP, HEAD_DIM).transpose(0, 2, 1, 3).reshape(D_MODEL, Q_W)
    w_in_b_p = jnp.concatenate([wq, wg], axis=1)
    w_out_b_p = w_out_b[0].astype(bf16).reshape(N_KV_HEADS, GROUP, HEAD_DIM, D_MODEL).transpose(1, 0, 2, 3).reshape(
        Q_W, D_MODEL)

    inv = ROPE_THETA ** (-jnp.arange(HALF, dtype=f32) / HALF)
    inv_row = jnp.pad(jnp.tile(inv, LANES // HALF), (0, D_MODEL - LANES))
    zeros_m = jnp.zeros((D_MODEL,), f32)
    p_model = jnp.stack([norm_pre_a[0], norm_post_a[0], norm_kv, norm_pre_b[0], norm_post_b[0], inv_row,
                         zeros_m, zeros_m]).astype(f32)
    p_rnn = jnp.concatenate([conv_w_a[0], conv_b_a, b_gate_r, b_gate_i, lru_lambda], axis=0).astype(f32)

    return [
        ("p_model", p_model),
        ("p_rnn", p_rnn),
        ("w_in_a", w_in_a[0].astype(bf16)),
        ("w_ri", w_ri),
        ("w_out_a", lane_pad(w_out_a[0].astype(bf16))),
        ("w_kv", w_kv_p),
        ("w_in_b", lane_pad(w_in_b_p)),
        ("sinks", attn_sinks[0].astype(f32)),
        ("w_out_b", lane_pad(w_out_b_p)),
    ]


def _k_to_cache_layout(k):
    B, R, _ = k.shape
    ka = k[..., :LANES].reshape(B, R, N_KV_HEADS, HALF)
    kb = k[..., LANES:].reshape(B, R, N_KV_HEADS, HALF)
    return jnp.concatenate([ka, kb], axis=-1)


def _k_from_cache_layout(k):
    B, R = k.shape[:2]
    return jnp.concatenate([k[..., :HALF].reshape(B, R, LANES), k[..., HALF:].reshape(B, R, LANES)], axis=-1)


def _to_streams(a, frames):
    NG, n, _, C = a.shape
    return a.reshape(NG, n, frames, NB, C).transpose(0, 3, 1, 2, 4).reshape(NG * NB, n * frames, C)


def kernel(x_prompt, x_sample, state_conv, state_rnn, cache_k, cache_v, norm_pre_a, w_in_a, conv_w_a, conv_b_a, w_gate_r, b_gate_r, w_gate_i, b_gate_i, lru_lambda, w_out_a, norm_post_a, norm_kv, w_kv, norm_pre_b, w_in_b, attn_sinks, w_out_b, norm_post_b):
    weights = _prep_weights(norm_pre_a, w_in_a, conv_w_a, conv_b_a, w_gate_r, b_gate_r, w_gate_i, b_gate_i,
                            lru_lambda, w_out_a, norm_post_a, norm_kv, w_kv, norm_pre_b, w_in_b,
                            attn_sinks, w_out_b, norm_post_b)
    tail = CONV_W - 1

    def unpack(y, t_out, h_out, k_o, v_o, frames):
        NG = t_out.shape[0]
        conv = t_out.reshape(NG, tail, NB, D_RNN).transpose(0, 2, 1, 3).reshape(NG * NB, tail, D_RNN)
        k = _k_to_cache_layout(_to_streams(k_o, frames))
        v = _to_streams(v_o, frames)
        return (y, conv[None], h_out.reshape(NG * NB, D_RNN)[None], k,
                v.reshape(v.shape[0], v.shape[1], N_KV_HEADS, HEAD_DIM))

    prompt = _run_trunk(x_prompt, None, weights, TT=CHUNK, pos0=0, n_kout=WINDOW // CHUNK, name="trunk_prompt")
    yp, pc, pr, pk, pv = unpack(*prompt, CHUNK)

    Bs, Ts, _ = x_sample.shape
    NGs = Bs // NB
    conv0 = state_conv[0].reshape(NGs, NB, tail, D_RNN).transpose(0, 2, 1, 3).reshape(NGs, TAIL, D_RNN)
    h0 = state_rnn[0].reshape(NGs, NB, D_RNN)
    kh0 = _k_from_cache_layout(cache_k)
    vh0 = cache_v.reshape(Bs, WINDOW, KV_W)
    sample = _run_trunk(x_sample, (conv0, h0, kh0, vh0), weights, TT=Ts, pos0=PAST_LEN, n_kout=1,
                        name="trunk_sample")
    ys, sc, sr, sk, sv = unpack(*sample, Ts)

    return (yp, ys, pc, pr, pk, pv, sc, sr, sk, sv)
```

```python
import functools

import numpy as np
import jax
import jax.numpy as jnp
from jax import lax
from jax.experimental import pallas as pl
from jax.experimental.pallas import tpu as pltpu

D_MODEL = 1024
D_RNN = 1408
N_RG_BLOCKS = 16
RG_BLOCK = D_RNN // N_RG_BLOCKS
CONV_W = 4
LRU_C = 8.0
N_HEADS = 16
N_KV_HEADS = 4
HEAD_DIM = 64
HALF = HEAD_DIM // 2
GROUP = N_HEADS // N_KV_HEADS
WINDOW = 128
CHUNK = 64
PAST_LEN = 1024
ROPE_THETA = 10000.0
EPS = 1e-6
LOG2E = 1.4426950408889634

KV_W = N_KV_HEADS * HEAD_DIM
Q_W = N_HEADS * HEAD_DIM

LANES = 128
SUBLANES = 8
MXU_DIM = 256

NB = SUBLANES
TAIL = (CONV_W - 1) * NB

GATE_K = 512
N_GATE_TILES = -(-D_RNN // MXU_DIM)

KEY_WIN = 256
N_PARTS = 1
W_PAD = LANES

P_PRE_A, P_POST_A, P_KV, P_PRE_B, P_POST_B, P_INV_FREQ = range(6)
P_CONV_B, P_B_R, P_B_I, P_LAMBDA = range(CONV_W, CONV_W + 4)


def _gate_windows():
    wins = []
    for j in range(N_GATE_TILES):
        c0 = j * MXU_DIM
        c1 = min(c0 + MXU_DIM, D_RNN)
        lo = (c0 // RG_BLOCK) * RG_BLOCK
        hi = ((c1 - 1) // RG_BLOCK + 1) * RG_BLOCK
        k0 = min((lo // LANES) * LANES, D_RNN - GATE_K)
        assert k0 <= lo and hi <= k0 + GATE_K
        wins.append((k0, c0, c1))
    return wins


GATE_WINDOWS = _gate_windows()


def _bdot(a, b):
    return jnp.dot(a, b, preferred_element_type=jnp.float32)


def _sigmoid(x):
    return 0.5 * jnp.tanh(0.5 * x) + 0.5


def _trunk_kernel(*refs, TT, NT, NG, has_state, pos0, n_kout, n_parts):
    it = iter(refs)
    x_hbm = next(it)
    if has_state:
        conv0_ref, h0_ref, kh0_ref, vh0_ref = next(it), next(it), next(it), next(it)
    p_model, p_rnn, w_in_a, w_ri, w_out_a, w_kv, w_in_b, sinks, w_out_b = [next(it) for _ in range(9)]
    y_hbm, tail_out, h_out, k_out, v_out = [next(it) for _ in range(5)]
    (in_buf, out_buf, sem_in, sem_out, sem_kv, ext_ref, tail_ref, hcar_ref, kbuf, vbuf,
     q_il, k_il, v_il, o_il, rope_small, cos_tab, sin_tab) = [next(it) for _ in range(17)]

    g_idx = pl.program_id(0)
    t_idx = pl.program_id(1)
    step = g_idx * NT + t_idx
    n_steps = NT * NG
    slot = step & 1
    R = TT * NB
    f32 = jnp.float32
    bf16 = jnp.bfloat16

    def in_copies(t, g, s):
        return [pltpu.make_async_copy(x_hbm.at[g * NB + b, pl.ds(t * TT, TT), :],
                                      in_buf.at[s, :, b, :], sem_in.at[s]) for b in range(NB)]

    def out_copies(t, g):
        return [pltpu.make_async_copy(out_buf.at[:, b, :],
                                      y_hbm.at[g * NB + b, pl.ds(t * TT, TT), :], sem_out.at[0]) for b in range(NB)]

    @pl.when(step == 0)
    def _():
        for c in in_copies(t_idx, g_idx, slot):
            c.start()

    @pl.when(step + 1 < n_steps)
    def _():
        nxt = step + 1
        for c in in_copies(nxt % NT, nxt // NT, 1 - slot):
            c.start()

    for c in in_copies(t_idx, g_idx, slot):
        c.wait()

    if has_state:
        @pl.when(t_idx == 0)
        def _():
            ext_ref[0:TAIL, :] = conv0_ref[...]
            hcar_ref[...] = h0_ref[...]
            zpad = jnp.zeros((NB, KEY_WIN - WINDOW, KV_W), bf16)
            kbuf[:, 0:WINDOW, :] = kh0_ref[...].astype(bf16)
            vbuf[:, 0:WINDOW, :] = vh0_ref[...].astype(bf16)
            kbuf[:, WINDOW:, :] = zpad
            vbuf[:, WINDOW:, :] = zpad
    else:
        @pl.when(t_idx == 0)
        def _():
            ext_ref[0:TAIL, :] = jnp.zeros((TAIL, D_RNN), f32)
            hcar_ref[...] = jnp.zeros((NB, D_RNN), f32)
            kbuf[...] = jnp.zeros(kbuf.shape, bf16)
            vbuf[...] = jnp.zeros(vbuf.shape, bf16)

    @pl.when(t_idx > 0)
    def _():
        ext_ref[0:TAIL, :] = tail_ref[...]

    f0 = pl.multiple_of(t_idx * TT, TT)

    @pl.when(g_idx == 0)
    def _():
        pos = (lax.broadcasted_iota(jnp.int32, (TT, LANES), 0) + (pos0 + t_idx * TT)).astype(f32)
        ang = pos * p_model[P_INV_FREQ:P_INV_FREQ + 1, :LANES]
        rope_small[0, pl.ds(f0, TT), :] = jnp.cos(ang)
        rope_small[1, pl.ds(f0, TT), :] = jnp.sin(ang)

    for t in range(TT):
        cos_tab[t * NB:(t + 1) * NB, :] = jnp.broadcast_to(rope_small[0, pl.ds(f0 + t, 1), :], (NB, LANES))
        sin_tab[t * NB:(t + 1) * NB, :] = jnp.broadcast_to(rope_small[1, pl.ds(f0 + t, 1), :], (NB, LANES))

    def rms_scale(v):
        return lax.rsqrt(jnp.sum(v * v, axis=-1, keepdims=True) * (1.0 / D_MODEL) + EPS)

    FP = TT // n_parts
    RP = FP * NB
    scale = HEAD_DIM ** -0.5 * LOG2E
    z = -p_rnn[P_LAMBDA:P_LAMBDA + 1, :]
    softplus = jnp.maximum(z, 0.0) + jnp.log1p(jnp.exp(-jnp.abs(z)))
    h = hcar_ref[...]
    x1_parts, gate2_parts = [], []
    for part in range(n_parts):
        r0 = part * RP

        x = in_buf[slot, part * FP:(part + 1) * FP].reshape(RP, D_MODEL)
        xn = (x * rms_scale(x) * p_model[P_PRE_A:P_PRE_A + 1, :]).astype(bf16)
        u = _bdot(xn, w_in_a[...])
        branch = u[:, :D_RNN]
        gate = u[:, D_RNN:]

        ext_ref[TAIL + r0:TAIL + r0 + RP, :] = branch
        conv = p_rnn[P_CONV_B:P_CONV_B + 1, :] + branch * p_rnn[CONV_W - 1:CONV_W, :]
        for tap in range(CONV_W - 1):
            conv = conv + ext_ref[tap * NB + r0:tap * NB + r0 + RP, :] * p_rnn[tap:tap + 1, :]
        if part == n_parts - 1:
            new_tail = branch[RP - TAIL:, :]
            tail_ref[...] = new_tail
            tail_out[...] = new_tail

        conv_bf = conv.astype(bf16)
        r_parts, i_parts = [], []
        for j, (k0, c0, c1) in enumerate(GATE_WINDOWS):
            rg = _bdot(conv_bf[:, k0:k0 + GATE_K], w_ri[j])
            r_parts.append(rg[:, :c1 - c0])
            i_parts.append(rg[:, MXU_DIM:MXU_DIM + c1 - c0])
        r_gate = _sigmoid(jnp.concatenate(r_parts, axis=1) + p_rnn[P_B_R:P_B_R + 1, :])
        i_gate = _sigmoid(jnp.concatenate(i_parts, axis=1) + p_rnn[P_B_I:P_B_I + 1, :])

        a = jnp.exp2((-LRU_C * LOG2E * softplus) * r_gate)
        one_m_a2 = 1.0 - a * a
        root = jnp.where(one_m_a2 > 0.0, one_m_a2 * lax.rsqrt(one_m_a2), 0.0)
        bvec = root * (i_gate * conv)

        hs = []
        for t in range(FP):
            h = a[t * NB:(t + 1) * NB, :] * h + bvec[t * NB:(t + 1) * NB, :]
            hs.append(h)
        h_all = jnp.concatenate(hs, axis=0)

        ya = _bdot((h_all * (gate * _sigmoid(gate))).astype(bf16), w_out_a[:, :D_MODEL])
        x1 = x + ya * rms_scale(ya) * p_model[P_POST_A:P_POST_A + 1, :]
        x1_parts.append(x1)

        x1s = x1 * rms_scale(x1)
        kv = _bdot((x1s * p_model[P_KV:P_KV + 1, :]).astype(bf16), w_kv[...])
        ub = _bdot((x1s * p_model[P_PRE_B:P_PRE_B + 1, :]).astype(bf16), w_in_b[:, :2 * Q_W])

        cos = cos_tab[r0:r0 + RP, :]
        sin = sin_tab[r0:r0 + RP, :]
        ka, kb, v = kv[:, :LANES], kv[:, LANES:2 * LANES], kv[:, 2 * LANES:]
        k_il[0, r0:r0 + RP, :] = ka * cos - kb * sin
        k_il[1, r0:r0 + RP, :] = kb * cos + ka * sin
        v_il[0, r0:r0 + RP, :] = v[:, :LANES]
        v_il[1, r0:r0 + RP, :] = v[:, LANES:]
        for j in range(GROUP):
            qa = ub[:, j * LANES:(j + 1) * LANES]
            qb = ub[:, Q_W // 2 + j * LANES:Q_W // 2 + (j + 1) * LANES]
            q_il[j, r0:r0 + RP, :] = (qa * cos - qb * sin) * scale
            q_il[GROUP + j, r0:r0 + RP, :] = (qb * cos + qa * sin) * scale
        gate2_parts.append(ub[:, Q_W:])
    hcar_ref[...] = h
    h_out[...] = h

    @pl.when(t_idx >= NT - n_kout)
    def _():
        tt = t_idx - (NT - n_kout)
        cps = []
        for l in range(KV_W // LANES):
            cps.append(pltpu.make_async_copy(k_il.at[l], k_out.at[g_idx, tt, :, pl.ds(l * LANES, LANES)],
                                             sem_kv.at[0]))
            cps.append(pltpu.make_async_copy(v_il.at[l], v_out.at[g_idx, tt, :, pl.ds(l * LANES, LANES)],
                                             sem_kv.at[1]))
        for c in cps:
            c.start()
        for c in cps:
            c.wait()

    if has_state:
        cur_off = WINDOW
        n_valid = WINDOW + TT
    else:
        ring = WINDOW // TT + 1
        cur_off = pl.multiple_of((t_idx % ring) * TT, TT)
        n_valid = jnp.minimum(t_idx + 1, ring) * TT

    lane_q = lax.broadcasted_iota(jnp.int32, (TT, 2 * LANES), 1)
    q_slot = (lane_q & (LANES - 1)) // HALF
    first_half = lax.broadcasted_iota(jnp.int32, (TT, LANES), 1) < HEAD_DIM
    invalid = lax.broadcasted_iota(jnp.int32, (TT, KEY_WIN), 1) >= n_valid
    col_row = lax.broadcasted_iota(jnp.int32, (1, KEY_WIN), 1)

    def attend_stream(b, carry):
        rows = pl.ds(b, TT, stride=NB)
        kbuf[b, pl.ds(cur_off, TT), :] = jnp.concatenate(
            [k_il[0, rows, :], k_il[1, rows, :]], axis=1).astype(bf16)
        vbuf[b, pl.ds(cur_off, TT), :] = jnp.concatenate(
            [v_il[0, rows, :], v_il[1, rows, :]], axis=1).astype(bf16)
        qcat = [jnp.concatenate([q_il[j, rows, :], q_il[GROUP + j, rows, :]], axis=1) for j in range(GROUP)]
        qstack = jnp.concatenate(
            [jnp.where(q_slot == h // GROUP, qcat[h % GROUP], 0.0) for h in range(N_HEADS)],
            axis=0).astype(bf16)
        s_all = lax.dot_general(qstack, kbuf[b], (((1,), (1,)), ((), ())),
                                preferred_element_type=f32)
        p_parts, inv_l = [], []
        for h in range(N_HEADS):
            fill = jnp.where(col_row == n_valid, sinks[h] * LOG2E, -jnp.inf)
            sh = jnp.where(invalid, fill, s_all[h * TT:(h + 1) * TT, :])
            ph = jnp.exp2(sh - jnp.max(sh, axis=-1, keepdims=True))
            inv_l.append(1.0 / jnp.sum(ph, axis=-1, keepdims=True))
            p_parts.append(ph.astype(bf16))
        o_g = [_bdot(jnp.concatenate(p_parts[g * GROUP:(g + 1) * GROUP], axis=0),
                     vbuf[b, :, (g // 2) * LANES:(g // 2 + 1) * LANES]) for g in range(N_KV_HEADS)]
        for j in range(GROUP):
            blk = [o_g[g][j * TT:(j + 1) * TT, :] * inv_l[g * GROUP + j] for g in range(N_KV_HEADS)]
            o_il[2 * j, rows, :] = jnp.where(first_half, blk[0], blk[1])
            o_il[2 * j + 1, rows, :] = jnp.where(first_half, blk[2], blk[3])
        return carry

    lax.fori_loop(0, NB, attend_stream, 0, unroll=8)

    @pl.when(step >= 1)
    def _():
        for c in out_copies(t_idx, g_idx):
            c.wait()

    for part in range(n_parts):
        r0 = part * RP
        o2 = jnp.concatenate([o_il[l, r0:r0 + RP, :] for l in range(Q_W // LANES)], axis=1)
        gate2 = gate2_parts[part]
        og = (o2 * (gate2 * _sigmoid(gate2))).astype(bf16)
        yb = _bdot(og, w_out_b[:, :D_MODEL])
        y = x1_parts[part] + yb * rms_scale(yb) * p_model[P_POST_B:P_POST_B + 1, :]
        out_buf[part * FP:(part + 1) * FP] = y.reshape(FP, NB, D_MODEL)
    for c in out_copies(t_idx, g_idx):
        c.start()

    @pl.when(step == n_steps - 1)
    def _():
        for c in out_copies(t_idx, g_idx):
            c.wait()


def _const_spec(shape):
    nd = len(shape)
    return pl.BlockSpec(shape, lambda g, t, _nd=nd: (0,) * _nd, pipeline_mode=pl.Buffered(1))


def _run_trunk(x, state, weights, *, TT, pos0, n_kout, name):
    B, S, _ = x.shape
    NT = S // TT
    NG = B // NB
    R = TT * NB
    assert NT * TT == S and NG * NB == B and n_kout <= NT
    assert state is not None or WINDOW % TT == 0
    has_state = state is not None

    in_specs = [pl.BlockSpec(memory_space=pl.ANY)]
    args = [x]
    if has_state:
        conv0, h0, kh0, vh0 = state
        in_specs += [
            pl.BlockSpec((None, TAIL, D_RNN), lambda g, t: (g, 0, 0)),
            pl.BlockSpec((None, NB, D_RNN), lambda g, t: (g, 0, 0)),
            pl.BlockSpec((NB, WINDOW, KV_W), lambda g, t: (g, 0, 0)),
            pl.BlockSpec((NB, WINDOW, KV_W), lambda g, t: (g, 0, 0)),
        ]
        args += [conv0, h0, kh0, vh0]
    for name_w, w in weights:
        if name_w == "sinks":
            in_specs.append(pl.BlockSpec(memory_space=pltpu.SMEM))
        else:
            in_specs.append(_const_spec(w.shape))
        args.append(w)

    out_shape = (
        jax.ShapeDtypeStruct((B, S, D_MODEL), jnp.float32),
        jax.ShapeDtypeStruct((NG, TAIL, D_RNN), jnp.float32),
        jax.ShapeDtypeStruct((NG, NB, D_RNN), jnp.float32),
        jax.ShapeDtypeStruct((NG, n_kout, R, KV_W), jnp.float32),
        jax.ShapeDtypeStruct((NG, n_kout, R, KV_W), jnp.float32),
    )
    out_specs = (
        pl.BlockSpec(memory_space=pl.ANY),
        pl.BlockSpec((None, TAIL, D_RNN), lambda g, t: (g, 0, 0)),
        pl.BlockSpec((None, NB, D_RNN), lambda g, t: (g, 0, 0)),
        pl.BlockSpec(memory_space=pl.ANY),
        pl.BlockSpec(memory_space=pl.ANY),
    )
    scratch = [
        pltpu.VMEM((2, TT, NB, D_MODEL), jnp.float32),
        pltpu.VMEM((TT, NB, D_MODEL), jnp.float32),
        pltpu.SemaphoreType.DMA((2,)),
        pltpu.SemaphoreType.DMA((1,)),
        pltpu.SemaphoreType.DMA((2,)),
        pltpu.VMEM((TAIL + R, D_RNN), jnp.float32),
        pltpu.VMEM((TAIL, D_RNN), jnp.float32),
        pltpu.VMEM((NB, D_RNN), jnp.float32),
        pltpu.VMEM((NB, KEY_WIN, KV_W), jnp.bfloat16),
        pltpu.VMEM((NB, KEY_WIN, KV_W), jnp.bfloat16),
        pltpu.VMEM((Q_W // LANES, R, LANES), jnp.float32),
        pltpu.VMEM((KV_W // LANES, R, LANES), jnp.float32),
        pltpu.VMEM((KV_W // LANES, R, LANES), jnp.float32),
        pltpu.VMEM((Q_W // LANES, R, LANES), jnp.float32),
        pltpu.VMEM((2, S, LANES), jnp.float32),
        pltpu.VMEM((R, LANES), jnp.float32),
        pltpu.VMEM((R, LANES), jnp.float32),
    ]
    kern = functools.partial(_trunk_kernel, TT=TT, NT=NT, NG=NG, has_state=has_state, pos0=pos0, n_kout=n_kout,
                             n_parts=N_PARTS if TT % (N_PARTS * 2) == 0 else 1)
    return pl.pallas_call(
        kern,
        grid=(NG, NT),
        in_specs=in_specs,
        out_specs=out_specs,
        out_shape=out_shape,
        scratch_shapes=scratch,
        compiler_params=pltpu.CompilerParams(
            dimension_semantics=("arbitrary", "arbitrary"),
            vmem_limit_bytes=58 * 1024 * 1024),
        name=name,
    )(*args)


def _prep_weights(norm_pre_a, w_in_a, conv_w_a, conv_b_a, w_gate_r, b_gate_r, w_gate_i, b_gate_i,
                  lru_lambda, w_out_a, norm_post_a, norm_kv, w_kv, norm_pre_b, w_in_b, attn_sinks,
                  w_out_b, norm_post_b):
    bf16 = jnp.bfloat16
    f32 = jnp.float32
    lane_pad = lambda w: jnp.pad(w, ((0, 0), (0, W_PAD)))

    w_g = jnp.stack([w_gate_r[0], w_gate_i[0]]).astype(bf16)
    dense = jnp.concatenate(
        [jnp.pad(w_g[:, n], ((0, 0), (0, 0), (RG_BLOCK * n, D_RNN - RG_BLOCK * (n + 1))))
         for n in range(N_RG_BLOCKS)], axis=1)
    tiles = []
    for k0, c0, c1 in GATE_WINDOWS:
        t = jnp.pad(dense[:, k0:k0 + GATE_K, c0:c1], ((0, 0), (0, 0), (0, MXU_DIM - (c1 - c0))))
        tiles.append(jnp.concatenate([t[0], t[1]], axis=1))
    w_ri = jnp.stack(tiles)

    wk = w_kv[:, :KV_W].astype(bf16).reshape(D_MODEL, N_KV_HEADS, 2, HALF)
    w_kv_p = jnp.concatenate([wk.transpose(0, 2, 1, 3).reshape(D_MODEL, KV_W), w_kv[:, KV_W:].astype(bf16)], axis=1)
    wb = w_in_b[0].astype(bf16)
    wq = wb[:, :Q_W].reshape(D_MODEL, N_KV_HEADS, GROUP, 2, HALF).transpose(0, 3, 2, 1, 4).reshape(D_MODEL, Q_W)
    wg = wb[:, Q_W:].reshape(D_MODEL, N_KV_HEADS, GROUP, HEAD_DIM).transpose(0, 2, 1, 3).reshape(D_MODEL, Q_W)
    w_in_b_p = jnp.concatenate([wq, wg], axis=1)
    w_out_b_p = w_out_b[0].astype(bf16).reshape(N_KV_HEADS, GROUP, HEAD_DIM, D_MODEL).transpose(1, 0, 2, 3).reshape(
        Q_W, D_MODEL)

    inv = ROPE_THETA ** (-jnp.arange(HALF, dtype=f32) / HALF)
    inv_row = jnp.pad(jnp.tile(inv, LANES // HALF), (0, D_MODEL - LANES))
    zeros_m = jnp.zeros((D_MODEL,), f32)
    p_model = jnp.stack([norm_pre_a[0], norm_post_a[0], norm_kv, norm_pre_b[0], norm_post_b[0], inv_row,
                         zeros_m, zeros_m]).astype(f32)
    p_rnn = jnp.concatenate([conv_w_a[0], conv_b_a, b_gate_r, b_gate_i, lru_lambda], axis=0).astype(f32)

    return [
        ("p_model", p_model),
        ("p_rnn", p_rnn),
        ("w_in_a", w_in_a[0].astype(bf16)),
        ("w_ri", w_ri),
        ("w_out_a", lane_pad(w_out_a[0].astype(bf16))),
        ("w_kv", w_kv_p),
        ("w_in_b", lane_pad(w_in_b_p)),
        ("sinks", attn_sinks[0].astype(f32)),
        ("w_out_b", lane_pad(w_out_b_p)),
    ]


def _k_to_cache_layout(k):
    B, R, _ = k.shape
    ka = k[..., :LANES].reshape(B, R, N_KV_HEADS, HALF)
    kb = k[..., LANES:].reshape(B, R, N_KV_HEADS, HALF)
    return jnp.concatenate([ka, kb], axis=-1)


def _k_from_cache_layout(k):
    B, R = k.shape[:2]
    return jnp.concatenate([k[..., :HALF].reshape(B, R, LANES), k[..., HALF:].reshape(B, R, LANES)], axis=-1)


def _to_streams(a, frames):
    NG, n, _, C = a.shape
    return a.reshape(NG, n, frames, NB, C).transpose(0, 3, 1, 2, 4).reshape(NG * NB, n * frames, C)


def kernel(x_prompt, x_sample, state_conv, state_rnn, cache_k, cache_v, norm_pre_a, w_in_a, conv_w_a, conv_b_a, w_gate_r, b_gate_r, w_gate_i, b_gate_i, lru_lambda, w_out_a, norm_post_a, norm_kv, w_kv, norm_pre_b, w_in_b, attn_sinks, w_out_b, norm_post_b):
    weights = _prep_weights(norm_pre_a, w_in_a, conv_w_a, conv_b_a, w_gate_r, b_gate_r, w_gate_i, b_gate_i,
                            lru_lambda, w_out_a, norm_post_a, norm_kv, w_kv, norm_pre_b, w_in_b,
                            attn_sinks, w_out_b, norm_post_b)
    tail = CONV_W - 1

    def unpack(y, t_out, h_out, k_o, v_o, frames):
        NG = t_out.shape[0]
        conv = t_out.reshape(NG, tail, NB, D_RNN).transpose(0, 2, 1, 3).reshape(NG * NB, tail, D_RNN)
        k = _k_to_cache_layout(_to_streams(k_o, frames))
        v = _to_streams(v_o, frames)
        return (y, conv[None], h_out.reshape(NG * NB, D_RNN)[None], k,
                v.reshape(v.shape[0], v.shape[1], N_KV_HEADS, HEAD_DIM))

    prompt = _run_trunk(x_prompt, None, weights, TT=CHUNK, pos0=0, n_kout=WINDOW // CHUNK, name="trunk_prompt")
    yp, pc, pr, pk, pv = unpack(*prompt, CHUNK)

    Bs, Ts, _ = x_sample.shape
    NGs = Bs // NB
    conv0 = state_conv[0].reshape(NGs, NB, tail, D_RNN).transpose(0, 2, 1, 3).reshape(NGs, TAIL, D_RNN)
    h0 = state_rnn[0].reshape(NGs, NB, D_RNN)
    kh0 = _k_from_cache_layout(cache_k)
    vh0 = cache_v.reshape(Bs, WINDOW, KV_W)
    sample = _run_trunk(x_sample, (conv0, h0, kh0, vh0), weights, TT=Ts, pos0=PAST_LEN, n_kout=1,
                        name="trunk_sample")
    ys, sc, sr, sk, sv = unpack(*sample, Ts)

    return (yp, ys, pc, pr, pk, pv, sc, sr, sk, sv)
```

```python
import functools

import numpy as np
import jax
import jax.numpy as jnp
from jax import lax
from jax.experimental import pallas as pl
from jax.experimental.pallas import tpu as pltpu

D_MODEL = 1024
D_RNN = 1408
N_RG_BLOCKS = 16
RG_BLOCK = D_RNN // N_RG_BLOCKS
CONV_W = 4
LRU_C = 8.0
N_HEADS = 16
N_KV_HEADS = 4
HEAD_DIM = 64
HALF = HEAD_DIM // 2
GROUP = N_HEADS // N_KV_HEADS
WINDOW = 128
CHUNK = 64
PAST_LEN = 1024
ROPE_THETA = 10000.0
EPS = 1e-6
LOG2E = 1.4426950408889634

KV_W = N_KV_HEADS * HEAD_DIM
Q_W = N_HEADS * HEAD_DIM

LANES = 128
SUBLANES = 8
MXU_DIM = 256

NB = SUBLANES
TAIL = (CONV_W - 1) * NB

GATE_K = 512
N_GATE_TILES = -(-D_RNN // MXU_DIM)

KEY_WIN = 256
N_PARTS = 1
W_PAD = LANES

P_PRE_A, P_POST_A, P_KV, P_PRE_B, P_POST_B, P_INV_FREQ = range(6)
P_CONV_B, P_B_R, P_B_I, P_LAMBDA = range(CONV_W, CONV_W + 4)


def _gate_windows():
    wins = []
    for j in range(N_GATE_TILES):
        c0 = j * MXU_DIM
        c1 = min(c0 + MXU_DIM, D_RNN)
        lo = (c0 // RG_BLOCK) * RG_BLOCK
        hi = ((c1 - 1) // RG_BLOCK + 1) * RG_BLOCK
        k0 = min((lo // LANES) * LANES, D_RNN - GATE_K)
        assert k0 <= lo and hi <= k0 + GATE_K
        wins.append((k0, c0, c1))
    return wins


GATE_WINDOWS = _gate_windows()


def _bdot(a, b):
    return jnp.dot(a, b, preferred_element_type=jnp.float32)


def _one_plus_tanh(half_x):
    return jnp.tanh(half_x) + 1.0


def _trunk_kernel(*refs, TT, NT, NG, has_state, pos0, n_kout, n_parts):
    it = iter(refs)
    x_hbm = next(it)
    if has_state:
        conv0_ref, h0_ref, kh0_ref, vh0_ref = next(it), next(it), next(it), next(it)
    p_model, p_rnn, w_in_a, w_ri, w_out_a, w_kv, w_in_b, sinks, w_out_b = [next(it) for _ in range(9)]
    y_hbm, tail_out, h_out, k_out, v_out = [next(it) for _ in range(5)]
    (in_buf, out_buf, sem_in, sem_out, sem_kv, ext_ref, tail_ref, hcar_ref, kbuf, vbuf,
     q_il, k_il, v_il, o_il, rope_small, cos_tab, sin_tab) = [next(it) for _ in range(17)]

    g_idx = pl.program_id(0)
    t_idx = pl.program_id(1)
    step = g_idx * NT + t_idx
    n_steps = NT * NG
    slot = step & 1
    R = TT * NB
    f32 = jnp.float32
    bf16 = jnp.bfloat16

    def in_copies(t, g, s):
        return [pltpu.make_async_copy(x_hbm.at[g * NB + b, pl.ds(t * TT, TT), :],
                                      in_buf.at[s, :, b, :], sem_in.at[s]) for b in range(NB)]

    def out_copies(t, g):
        return [pltpu.make_async_copy(out_buf.at[:, b, :],
                                      y_hbm.at[g * NB + b, pl.ds(t * TT, TT), :], sem_out.at[0]) for b in range(NB)]

    @pl.when(step == 0)
    def _():
        for c in in_copies(t_idx, g_idx, slot):
            c.start()

    @pl.when(step + 1 < n_steps)
    def _():
        nxt = step + 1
        for c in in_copies(nxt % NT, nxt // NT, 1 - slot):
            c.start()

    for c in in_copies(t_idx, g_idx, slot):
        c.wait()

    if has_state:
        @pl.when(t_idx == 0)
        def _():
            ext_ref[0:TAIL, :] = conv0_ref[...]
            hcar_ref[...] = 2.0 * h0_ref[...]
            zpad = jnp.zeros((NB, KEY_WIN - WINDOW, KV_W), bf16)
            kbuf[:, 0:WINDOW, :] = kh0_ref[...].astype(bf16)
            vbuf[:, 0:WINDOW, :] = vh0_ref[...].astype(bf16)
            kbuf[:, WINDOW:, :] = zpad
            vbuf[:, WINDOW:, :] = zpad
    else:
        @pl.when(t_idx == 0)
        def _():
            ext_ref[0:TAIL, :] = jnp.zeros((TAIL, D_RNN), f32)
            hcar_ref[...] = jnp.zeros((NB, D_RNN), f32)
            kbuf[...] = jnp.zeros(kbuf.shape, bf16)
            vbuf[...] = jnp.zeros(vbuf.shape, bf16)

    @pl.when(t_idx > 0)
    def _():
        ext_ref[0:TAIL, :] = tail_ref[...]

    f0 = pl.multiple_of(t_idx * TT, TT)

    @pl.when(g_idx == 0)
    def _():
        pos = (lax.broadcasted_iota(jnp.int32, (TT, LANES), 0) + (pos0 + t_idx * TT)).astype(f32)
        ang = pos * p_model[P_INV_FREQ:P_INV_FREQ + 1, :LANES]
        rope_small[0, pl.ds(f0, TT), :] = jnp.cos(ang)
        rope_small[1, pl.ds(f0, TT), :] = jnp.sin(ang)

    for t in range(TT):
        cos_tab[t * NB:(t + 1) * NB, :] = jnp.broadcast_to(rope_small[0, pl.ds(f0 + t, 1), :], (NB, LANES))
        sin_tab[t * NB:(t + 1) * NB, :] = jnp.broadcast_to(rope_small[1, pl.ds(f0 + t, 1), :], (NB, LANES))

    def rms_scale(v):
        return lax.rsqrt(jnp.sum(v * v, axis=-1, keepdims=True) * (1.0 / D_MODEL) + EPS)

    FP = TT // n_parts
    RP = FP * NB
    scale = HEAD_DIM ** -0.5 * LOG2E
    z = -p_rnn[P_LAMBDA:P_LAMBDA + 1, :]
    softplus = jnp.maximum(z, 0.0) + jnp.log1p(jnp.exp(-jnp.abs(z)))
    half_log2_a = (-0.5 * LRU_C * LOG2E) * softplus
    h = hcar_ref[...]
    x1_parts, gate2_parts = [], []
    for part in range(n_parts):
        r0 = part * RP

        x = in_buf[slot, part * FP:(part + 1) * FP].reshape(RP, D_MODEL)
        xn = (x * rms_scale(x) * p_model[P_PRE_A:P_PRE_A + 1, :]).astype(bf16)
        u = _bdot(xn, w_in_a[...])
        branch = u[:, :D_RNN]
        gate = u[:, D_RNN:]

        ext_ref[TAIL + r0:TAIL + r0 + RP, :] = branch
        conv = p_rnn[P_CONV_B:P_CONV_B + 1, :] + branch * p_rnn[CONV_W - 1:CONV_W, :]
        for tap in range(CONV_W - 1):
            conv = conv + ext_ref[tap * NB + r0:tap * NB + r0 + RP, :] * p_rnn[tap:tap + 1, :]
        if part == n_parts - 1:
            new_tail = branch[RP - TAIL:, :]
            tail_ref[...] = new_tail
            tail_out[...] = new_tail

        conv_bf = conv.astype(bf16)
        r_parts, i_parts = [], []
        for j, (k0, c0, c1) in enumerate(GATE_WINDOWS):
            rg = _bdot(conv_bf[:, k0:k0 + GATE_K], w_ri[j])
            r_parts.append(rg[:, :c1 - c0])
            i_parts.append(rg[:, MXU_DIM:MXU_DIM + c1 - c0])
        tr = jnp.tanh(jnp.concatenate(r_parts, axis=1) + p_rnn[P_B_R:P_B_R + 1, :])
        i2 = _one_plus_tanh(jnp.concatenate(i_parts, axis=1) + p_rnn[P_B_I:P_B_I + 1, :])

        a = jnp.exp2(half_log2_a * tr + half_log2_a)
        one_m_a2 = 1.0 - a * a
        root = jnp.where(one_m_a2 > 0.0, one_m_a2 * lax.rsqrt(one_m_a2), 0.0)
        b2 = root * (i2 * conv)

        hs = []
        for t in range(FP):
            h = a[t * NB:(t + 1) * NB, :] * h + b2[t * NB:(t + 1) * NB, :]
            hs.append(h)
        h_all = jnp.concatenate(hs, axis=0)

        ya = _bdot((h_all * (gate * _one_plus_tanh(gate))).astype(bf16), w_out_a[:, :D_MODEL])
        x1 = x + ya * rms_scale(ya) * p_model[P_POST_A:P_POST_A + 1, :]
        x1_parts.append(x1)

        x1s = x1 * rms_scale(x1)
        kv = _bdot((x1s * p_model[P_KV:P_KV + 1, :]).astype(bf16), w_kv[...])
        ub = _bdot((x1s * p_model[P_PRE_B:P_PRE_B + 1, :]).astype(bf16), w_in_b[:, :2 * Q_W])

        cos = cos_tab[r0:r0 + RP, :]
        sin = sin_tab[r0:r0 + RP, :]
        ka, kb, v = kv[:, :LANES], kv[:, LANES:2 * LANES], kv[:, 2 * LANES:]
        k_il[0, r0:r0 + RP, :] = ka * cos - kb * sin
        k_il[1, r0:r0 + RP, :] = kb * cos + ka * sin
        v_il[0, r0:r0 + RP, :] = v[:, :LANES]
        v_il[1, r0:r0 + RP, :] = v[:, LANES:]
        for j in range(GROUP):
            qa = ub[:, j * LANES:(j + 1) * LANES]
            qb = ub[:, Q_W // 2 + j * LANES:Q_W // 2 + (j + 1) * LANES]
            q_il[j, r0:r0 + RP, :] = (qa * cos - qb * sin) * scale
            q_il[GROUP + j, r0:r0 + RP, :] = (qb * cos + qa * sin) * scale
        gate2_parts.append(ub[:, Q_W:])
    hcar_ref[...] = h
    h_out[...] = 0.5 * h

    @pl.when(t_idx >= NT - n_kout)
    def _():
        tt = t_idx - (NT - n_kout)
        cps = []
        for l in range(KV_W // LANES):
            cps.append(pltpu.make_async_copy(k_il.at[l], k_out.at[g_idx, tt, :, pl.ds(l * LANES, LANES)],
                                             sem_kv.at[0]))
            cps.append(pltpu.make_async_copy(v_il.at[l], v_out.at[g_idx, tt, :, pl.ds(l * LANES, LANES)],
                                             sem_kv.at[1]))
        for c in cps:
            c.start()
        for c in cps:
            c.wait()

    if has_state:
        cur_off = WINDOW
        n_valid = WINDOW + TT
    else:
        ring = WINDOW // TT + 1
        cur_off = pl.multiple_of((t_idx % ring) * TT, TT)
        n_valid = jnp.minimum(t_idx + 1, ring) * TT

    lane_q = lax.broadcasted_iota(jnp.int32, (TT, 2 * LANES), 1)
    q_slot = (lane_q & (LANES - 1)) // HALF
    first_half = lax.broadcasted_iota(jnp.int32, (TT, LANES), 1) < HEAD_DIM
    invalid = lax.broadcasted_iota(jnp.int32, (TT, KEY_WIN), 1) >= n_valid
    col_row = lax.broadcasted_iota(jnp.int32, (1, KEY_WIN), 1)

    def attend_stream(b, carry):
        rows = pl.ds(b, TT, stride=NB)
        kbuf[b, pl.ds(cur_off, TT), :] = jnp.concatenate(
            [k_il[0, rows, :], k_il[1, rows, :]], axis=1).astype(bf16)
        vbuf[b, pl.ds(cur_off, TT), :] = jnp.concatenate(
            [v_il[0, rows, :], v_il[1, rows, :]], axis=1).astype(bf16)
        qcat = [jnp.concatenate([q_il[j, rows, :], q_il[GROUP + j, rows, :]], axis=1) for j in range(GROUP)]
        qstack = jnp.concatenate(
            [jnp.where(q_slot == h // GROUP, qcat[h % GROUP], 0.0) for h in range(N_HEADS)],
            axis=0).astype(bf16)
        s_all = lax.dot_general(qstack, kbuf[b], (((1,), (1,)), ((), ())),
                                preferred_element_type=f32)
        p_parts, row_sum = [], []
        for h in range(N_HEADS):
            fill = jnp.where(col_row == n_valid, sinks[h] * LOG2E, -jnp.inf)
            sh = jnp.where(invalid, fill, s_all[h * TT:(h + 1) * TT, :])
            ph = jnp.exp2(sh - jnp.max(sh, axis=-1, keepdims=True))
            row_sum.append(jnp.sum(ph, axis=-1, keepdims=True))
            p_parts.append(ph.astype(bf16))
        o_g = [_bdot(jnp.concatenate(p_parts[g * GROUP:(g + 1) * GROUP], axis=0),
                     vbuf[b, :, (g // 2) * LANES:(g // 2 + 1) * LANES]) for g in range(N_KV_HEADS)]
        for j in range(GROUP):
            for half in range(2):
                g0, g1 = 2 * half, 2 * half + 1
                num = jnp.where(first_half, o_g[g0][j * TT:(j + 1) * TT, :], o_g[g1][j * TT:(j + 1) * TT, :])
                den = jnp.where(first_half, row_sum[g0 * GROUP + j], row_sum[g1 * GROUP + j])
                o_il[2 * j + half, rows, :] = num * (1.0 / den)
        return carry

    lax.fori_loop(0, NB, attend_stream, 0, unroll=8)

    @pl.when(step >= 1)
    def _():
        for c in out_copies(t_idx, g_idx):
            c.wait()

    for part in range(n_parts):
        r0 = part * RP
        o2 = jnp.concatenate([o_il[l, r0:r0 + RP, :] for l in range(Q_W // LANES)], axis=1)
        gate2 = gate2_parts[part]
        og = (o2 * (gate2 * _one_plus_tanh(gate2))).astype(bf16)
        yb = _bdot(og, w_out_b[:, :D_MODEL])
        y = x1_parts[part] + yb * rms_scale(yb) * p_model[P_POST_B:P_POST_B + 1, :]
        out_buf[part * FP:(part + 1) * FP] = y.reshape(FP, NB, D_MODEL)
    for c in out_copies(t_idx, g_idx):
        c.start()

    @pl.when(step == n_steps - 1)
    def _():
        for c in out_copies(t_idx, g_idx):
            c.wait()


def _const_spec(shape):
    nd = len(shape)
    return pl.BlockSpec(shape, lambda g, t, _nd=nd: (0,) * _nd, pipeline_mode=pl.Buffered(1))


def _run_trunk(x, state, weights, *, TT, pos0, n_kout, name):
    B, S, _ = x.shape
    NT = S // TT
    NG = B // NB
    R = TT * NB
    assert NT * TT == S and NG * NB == B and n_kout <= NT
    assert state is not None or WINDOW % TT == 0
    has_state = state is not None

    in_specs = [pl.BlockSpec(memory_space=pl.ANY)]
    args = [x]
    if has_state:
        conv0, h0, kh0, vh0 = state
        in_specs += [
            pl.BlockSpec((None, TAIL, D_RNN), lambda g, t: (g, 0, 0)),
            pl.BlockSpec((None, NB, D_RNN), lambda g, t: (g, 0, 0)),
            pl.BlockSpec((NB, WINDOW, KV_W), lambda g, t: (g, 0, 0)),
            pl.BlockSpec((NB, WINDOW, KV_W), lambda g, t: (g, 0, 0)),
        ]
        args += [conv0, h0, kh0, vh0]
    for name_w, w in weights:
        if name_w == "sinks":
            in_specs.append(pl.BlockSpec(memory_space=pltpu.SMEM))
        else:
            in_specs.append(_const_spec(w.shape))
        args.append(w)

    out_shape = (
        jax.ShapeDtypeStruct((B, S, D_MODEL), jnp.float32),
        jax.ShapeDtypeStruct((NG, TAIL, D_RNN), jnp.float32),
        jax.ShapeDtypeStruct((NG, NB, D_RNN), jnp.float32),
        jax.ShapeDtypeStruct((NG, n_kout, R, KV_W), jnp.float32),
        jax.ShapeDtypeStruct((NG, n_kout, R, KV_W), jnp.float32),
    )
    out_specs = (
        pl.BlockSpec(memory_space=pl.ANY),
        pl.BlockSpec((None, TAIL, D_RNN), lambda g, t: (g, 0, 0)),
        pl.BlockSpec((None, NB, D_RNN), lambda g, t: (g, 0, 0)),
        pl.BlockSpec(memory_space=pl.ANY),
        pl.BlockSpec(memory_space=pl.ANY),
    )
    scratch = [
        pltpu.VMEM((2, TT, NB, D_MODEL), jnp.float32),
        pltpu.VMEM((TT, NB, D_MODEL), jnp.float32),
        pltpu.SemaphoreType.DMA((2,)),
        pltpu.SemaphoreType.DMA((1,)),
        pltpu.SemaphoreType.DMA((2,)),
        pltpu.VMEM((TAIL + R, D_RNN), jnp.float32),
        pltpu.VMEM((TAIL, D_RNN), jnp.float32),
        pltpu.VMEM((NB, D_RNN), jnp.float32),
        pltpu.VMEM((NB, KEY_WIN, KV_W), jnp.bfloat16),
        pltpu.VMEM((NB, KEY_WIN, KV_W), jnp.bfloat16),
        pltpu.VMEM((Q_W // LANES, R, LANES), jnp.float32),
        pltpu.VMEM((KV_W // LANES, R, LANES), jnp.float32),
        pltpu.VMEM((KV_W // LANES, R, LANES), jnp.float32),
        pltpu.VMEM((Q_W // LANES, R, LANES), jnp.float32),
        pltpu.VMEM((2, S, LANES), jnp.float32),
        pltpu.VMEM((R, LANES), jnp.float32),
        pltpu.VMEM((R, LANES), jnp.float32),
    ]
    kern = functools.partial(_trunk_kernel, TT=TT, NT=NT, NG=NG, has_state=has_state, pos0=pos0, n_kout=n_kout,
                             n_parts=N_PARTS if TT % (N_PARTS * 2) == 0 else 1)
    return pl.pallas_call(
        kern,
        grid=(NG, NT),
        in_specs=in_specs,
        out_specs=out_specs,
        out_shape=out_shape,
        scratch_shapes=scratch,
        compiler_params=pltpu.CompilerParams(
            dimension_semantics=("arbitrary", "arbitrary"),
            vmem_limit_bytes=58 * 1024 * 1024),
        name=name,
    )(*args)


def _prep_weights(norm_pre_a, w_in_a, conv_w_a, conv_b_a, w_gate_r, b_gate_r, w_gate_i, b_gate_i,
                  lru_lambda, w_out_a, norm_post_a, norm_kv, w_kv, norm_pre_b, w_in_b, attn_sinks,
                  w_out_b, norm_post_b):
    bf16 = jnp.bfloat16
    f32 = jnp.float32
    lane_pad = lambda w: jnp.pad(w, ((0, 0), (0, W_PAD)))

    w_g = (0.5 * jnp.stack([w_gate_r[0], w_gate_i[0]])).astype(bf16)
    dense = jnp.concatenate(
        [jnp.pad(w_g[:, n], ((0, 0), (0, 0), (RG_BLOCK * n, D_RNN - RG_BLOCK * (n + 1))))
         for n in range(N_RG_BLOCKS)], axis=1)
    tiles = []
    for k0, c0, c1 in GATE_WINDOWS:
        t = jnp.pad(dense[:, k0:k0 + GATE_K, c0:c1], ((0, 0), (0, 0), (0, MXU_DIM - (c1 - c0))))
        tiles.append(jnp.concatenate([t[0], t[1]], axis=1))
    w_ri = jnp.stack(tiles)

    wk = w_kv[:, :KV_W].astype(bf16).reshape(D_MODEL, N_KV_HEADS, 2, HALF)
    w_kv_p = jnp.concatenate([wk.transpose(0, 2, 1, 3).reshape(D_MODEL, KV_W), w_kv[:, KV_W:].astype(bf16)], axis=1)
    wb = w_in_b[0].astype(bf16)
    wq = wb[:, :Q_W].reshape(D_MODEL, N_KV_HEADS, GROUP, 2, HALF).transpose(0, 3, 2, 1, 4).reshape(D_MODEL, Q_W)
    wg = (0.5 * wb[:, Q_W:]).reshape(D_MODEL, N_KV_HEADS, GROUP, HEAD_DIM).transpose(0, 2, 1, 3).reshape(D_MODEL, Q_W)
    w_in_b_p = jnp.concatenate([wq, wg], axis=1)
    w_out_b_p = w_out_b[0].astype(bf16).reshape(N_KV_HEADS, GROUP, HEAD_DIM, D_MODEL).transpose(1, 0, 2, 3).reshape(
        Q_W, D_MODEL)

    inv = ROPE_THETA ** (-jnp.arange(HALF, dtype=f32) / HALF)
    inv_row = jnp.pad(jnp.tile(inv, LANES // HALF), (0, D_MODEL - LANES))
    zeros_m = jnp.zeros((D_MODEL,), f32)
    p_model = jnp.stack([norm_pre_a[0], norm_post_a[0], norm_kv, norm_pre_b[0], norm_post_b[0], inv_row,
                         zeros_m, zeros_m]).astype(f32)
    p_rnn = jnp.concatenate([conv_w_a[0], conv_b_a, 0.5 * b_gate_r, 0.5 * b_gate_i, lru_lambda], axis=0).astype(f32)
    in_a_scale = jnp.concatenate([jnp.ones((D_RNN,), f32), jnp.full((D_RNN,), 0.5, f32)])

    return [
        ("p_model", p_model),
        ("p_rnn", p_rnn),
        ("w_in_a", (w_in_a[0] * in_a_scale).astype(bf16)),
        ("w_ri", w_ri),
        ("w_out_a", lane_pad((0.5 * w_out_a[0]).astype(bf16))),
        ("w_kv", w_kv_p),
        ("w_in_b", lane_pad(w_in_b_p)),
        ("sinks", attn_sinks[0].astype(f32)),
        ("w_out_b", lane_pad(w_out_b_p)),
    ]


def _k_to_cache_layout(k):
    B, R, _ = k.shape
    ka = k[..., :LANES].reshape(B, R, N_KV_HEADS, HALF)
    kb = k[..., LANES:].reshape(B, R, N_KV_HEADS, HALF)
    return jnp.concatenate([ka, kb], axis=-1)


def _k_from_cache_layout(k):
    B, R = k.shape[:2]
    return jnp.concatenate([k[..., :HALF].reshape(B, R, LANES), k[..., HALF:].reshape(B, R, LANES)], axis=-1)


def _to_streams(a, frames):
    NG, n, _, C = a.shape
    return a.reshape(NG, n, frames, NB, C).transpose(0, 3, 1, 2, 4).reshape(NG * NB, n * frames, C)


def kernel(x_prompt, x_sample, state_conv, state_rnn, cache_k, cache_v, norm_pre_a, w_in_a, conv_w_a, conv_b_a, w_gate_r, b_gate_r, w_gate_i, b_gate_i, lru_lambda, w_out_a, norm_post_a, norm_kv, w_kv, norm_pre_b, w_in_b, attn_sinks, w_out_b, norm_post_b):
    weights = _prep_weights(norm_pre_a, w_in_a, conv_w_a, conv_b_a, w_gate_r, b_gate_r, w_gate_i, b_gate_i,
                            lru_lambda, w_out_a, norm_post_a, norm_kv, w_kv, norm_pre_b, w_in_b,
                            attn_sinks, w_out_b, norm_post_b)
    tail = CONV_W - 1

    def unpack(y, t_out, h_out, k_o, v_o, frames):
        NG = t_out.shape[0]
        conv = t_out.reshape(NG, tail, NB, D_RNN).transpose(0, 2, 1, 3).reshape(NG * NB, tail, D_RNN)
        k = _k_to_cache_layout(_to_streams(k_o, frames))
        v = _to_streams(v_o, frames)
        return (y, conv[None], h_out.reshape(NG * NB, D_RNN)[None], k,
                v.reshape(v.shape[0], v.shape[1], N_KV_HEADS, HEAD_DIM))

    prompt = _run_trunk(x_prompt, None, weights, TT=CHUNK, pos0=0, n_kout=WINDOW // CHUNK, name="trunk_prompt")
    yp, pc, pr, pk, pv = unpack(*prompt, CHUNK)

    Bs, Ts, _ = x_sample.shape
    NGs = Bs // NB
    conv0 = state_conv[0].reshape(NGs, NB, tail, D_RNN).transpose(0, 2, 1, 3).reshape(NGs, TAIL, D_RNN)
    h0 = state_rnn[0].reshape(NGs, NB, D_RNN)
    kh0 = _k_from_cache_layout(cache_k)
    vh0 = cache_v.reshape(Bs, WINDOW, KV_W)
    sample = _run_trunk(x_sample, (conv0, h0, kh0, vh0), weights, TT=Ts, pos0=PAST_LEN, n_kout=1,
                        name="trunk_sample")
    ys, sc, sr, sk, sv = unpack(*sample, Ts)

    return (yp, ys, pc, pr, pk, pv, sc, sr, sk, sv)
```

```python
import functools

import numpy as np
import jax
import jax.numpy as jnp
from jax import lax
from jax.experimental import pallas as pl
from jax.experimental.pallas import tpu as pltpu

D_MODEL = 1024
D_RNN = 1408
N_RG_BLOCKS = 16
RG_BLOCK = D_RNN // N_RG_BLOCKS
CONV_W = 4
LRU_C = 8.0
N_HEADS = 16
N_KV_HEADS = 4
HEAD_DIM = 64
HALF = HEAD_DIM // 2
GROUP = N_HEADS // N_KV_HEADS
WINDOW = 128
CHUNK = 64
PAST_LEN = 1024
ROPE_THETA = 10000.0
EPS = 1e-6
LOG2E = 1.4426950408889634

KV_W = N_KV_HEADS * HEAD_DIM
Q_W = N_HEADS * HEAD_DIM

LANES = 128
SUBLANES = 8
MXU_DIM = 256

NB = SUBLANES
TAIL = (CONV_W - 1) * NB

GATE_K = 512
N_GATE_TILES = -(-D_RNN // MXU_DIM)

KEY_WIN = 256
N_PARTS = 1
W_PAD = LANES

P_PRE_A, P_POST_A, P_KV, P_PRE_B, P_POST_B, P_INV_FREQ = range(6)
P_CONV_B, P_B_R, P_B_I, P_LAMBDA = range(CONV_W, CONV_W + 4)


def _gate_windows():
    wins = []
    for j in range(N_GATE_TILES):
        c0 = j * MXU_DIM
        c1 = min(c0 + MXU_DIM, D_RNN)
        lo = (c0 // RG_BLOCK) * RG_BLOCK
        hi = ((c1 - 1) // RG_BLOCK + 1) * RG_BLOCK
        k0 = min((lo // LANES) * LANES, D_RNN - GATE_K)
        assert k0 <= lo and hi <= k0 + GATE_K
        wins.append((k0, c0, c1))
    return wins


GATE_WINDOWS = _gate_windows()


def _bdot(a, b):
    return jnp.dot(a, b, preferred_element_type=jnp.float32)


def _one_plus_tanh(half_x):
    return jnp.tanh(half_x) + 1.0


def _trunk_kernel(*refs, TT, NT, NG, has_state, pos0, n_kout, n_parts):
    it = iter(refs)
    x_hbm = next(it)
    if has_state:
        conv0_ref, h0_ref, kh0_ref, vh0_ref = next(it), next(it), next(it), next(it)
    p_model, p_rnn, w_in_a, w_ri, w_out_a, w_kv, w_in_b, sinks, w_out_b = [next(it) for _ in range(9)]
    y_hbm, tail_out, h_out, k_out, v_out = [next(it) for _ in range(5)]
    (in_buf, out_buf, sem_in, sem_out, sem_kv, ext_ref, tail_ref, hcar_ref, kbuf, vbuf,
     q_il, k_il, v_il, o_il, rope_small, cos_tab, sin_tab) = [next(it) for _ in range(17)]

    g_idx = pl.program_id(0)
    t_idx = pl.program_id(1)
    step = g_idx * NT + t_idx
    n_steps = NT * NG
    slot = step & 1
    R = TT * NB
    f32 = jnp.float32
    bf16 = jnp.bfloat16

    def in_copies(t, g, s):
        return [pltpu.make_async_copy(x_hbm.at[g * NB + b, pl.ds(t * TT, TT), :],
                                      in_buf.at[s, :, b, :], sem_in.at[s]) for b in range(NB)]

    def out_copies(t, g):
        return [pltpu.make_async_copy(out_buf.at[:, b, :],
                                      y_hbm.at[g * NB + b, pl.ds(t * TT, TT), :], sem_out.at[0]) for b in range(NB)]

    @pl.when(step == 0)
    def _():
        for c in in_copies(t_idx, g_idx, slot):
            c.start()

    @pl.when(step + 1 < n_steps)
    def _():
        nxt = step + 1
        for c in in_copies(nxt % NT, nxt // NT, 1 - slot):
            c.start()

    for c in in_copies(t_idx, g_idx, slot):
        c.wait()

    if has_state:
        @pl.when(t_idx == 0)
        def _():
            ext_ref[0:TAIL, :] = conv0_ref[...]
            hcar_ref[...] = 2.0 * h0_ref[...]
            zpad = jnp.zeros((NB, KEY_WIN - WINDOW, KV_W), bf16)
            kbuf[:, 0:WINDOW, :] = kh0_ref[...].astype(bf16)
            vbuf[:, 0:WINDOW, :] = vh0_ref[...].astype(bf16)
            kbuf[:, WINDOW:, :] = zpad
            vbuf[:, WINDOW:, :] = zpad
    else:
        @pl.when(t_idx == 0)
        def _():
            ext_ref[0:TAIL, :] = jnp.zeros((TAIL, D_RNN), f32)
            hcar_ref[...] = jnp.zeros((NB, D_RNN), f32)
            kbuf[...] = jnp.zeros(kbuf.shape, bf16)
            vbuf[...] = jnp.zeros(vbuf.shape, bf16)

    @pl.when(t_idx > 0)
    def _():
        ext_ref[0:TAIL, :] = tail_ref[...]

    f0 = pl.multiple_of(t_idx * TT, TT)

    @pl.when(g_idx == 0)
    def _():
        pos = (lax.broadcasted_iota(jnp.int32, (TT, LANES), 0) + (pos0 + t_idx * TT)).astype(f32)
        ang = pos * p_model[P_INV_FREQ:P_INV_FREQ + 1, :LANES]
        rope_small[0, pl.ds(f0, TT), :] = jnp.cos(ang)
        rope_small[1, pl.ds(f0, TT), :] = jnp.sin(ang)

    for t in range(TT):
        cos_tab[t * NB:(t + 1) * NB, :] = jnp.broadcast_to(rope_small[0, pl.ds(f0 + t, 1), :], (NB, LANES))
        sin_tab[t * NB:(t + 1) * NB, :] = jnp.broadcast_to(rope_small[1, pl.ds(f0 + t, 1), :], (NB, LANES))

    def rms_scale(v):
        return lax.rsqrt(jnp.sum(v * v, axis=-1, keepdims=True) * (1.0 / D_MODEL) + EPS)

    FP = TT // n_parts
    RP = FP * NB
    scale = HEAD_DIM ** -0.5 * LOG2E
    z = -p_rnn[P_LAMBDA:P_LAMBDA + 1, :]
    softplus = jnp.maximum(z, 0.0) + jnp.log1p(jnp.exp(-jnp.abs(z)))
    half_log2_a = (-0.5 * LRU_C * LOG2E) * softplus
    h = hcar_ref[...]
    x1_parts, gate2_parts = [], []
    for part in range(n_parts):
        r0 = part * RP

        x = in_buf[slot, part * FP:(part + 1) * FP].reshape(RP, D_MODEL)
        xg = (x * p_model[P_PRE_A:P_PRE_A + 1, :]).astype(bf16)
        u = _bdot(xg, w_in_a[...]) * rms_scale(x)
        branch = u[:, :D_RNN]
        gate = u[:, D_RNN:]

        ext_ref[TAIL + r0:TAIL + r0 + RP, :] = branch
        conv = p_rnn[P_CONV_B:P_CONV_B + 1, :] + branch * p_rnn[CONV_W - 1:CONV_W, :]
        for tap in range(CONV_W - 1):
            conv = conv + ext_ref[tap * NB + r0:tap * NB + r0 + RP, :] * p_rnn[tap:tap + 1, :]
        if part == n_parts - 1:
            new_tail = branch[RP - TAIL:, :]
            tail_ref[...] = new_tail
            tail_out[...] = new_tail

        conv_bf = conv.astype(bf16)
        r_parts, i_parts = [], []
        for j, (k0, c0, c1) in enumerate(GATE_WINDOWS):
            rg = _bdot(conv_bf[:, k0:k0 + GATE_K], w_ri[j])
            r_parts.append(rg[:, :c1 - c0])
            i_parts.append(rg[:, MXU_DIM:MXU_DIM + c1 - c0])
        tr = jnp.tanh(jnp.concatenate(r_parts, axis=1) + p_rnn[P_B_R:P_B_R + 1, :])
        i2 = _one_plus_tanh(jnp.concatenate(i_parts, axis=1) + p_rnn[P_B_I:P_B_I + 1, :])

        a = jnp.exp2(half_log2_a * tr + half_log2_a)
        one_m_a2 = 1.0 - a * a
        root = jnp.where(one_m_a2 > 0.0, one_m_a2 * lax.rsqrt(one_m_a2), 0.0)
        b2 = root * (i2 * conv)

        hs = []
        for t in range(FP):
            h = a[t * NB:(t + 1) * NB, :] * h + b2[t * NB:(t + 1) * NB, :]
            hs.append(h)
        h_all = jnp.concatenate(hs, axis=0)

        ya = _bdot((h_all * (gate * _one_plus_tanh(gate))).astype(bf16), w_out_a[:, :D_MODEL])
        x1 = x + ya * rms_scale(ya) * p_model[P_POST_A:P_POST_A + 1, :]
        x1_parts.append(x1)

        rs1 = rms_scale(x1)
        kv = _bdot((x1 * p_model[P_KV:P_KV + 1, :]).astype(bf16), w_kv[...]) * rs1
        ub = _bdot((x1 * p_model[P_PRE_B:P_PRE_B + 1, :]).astype(bf16), w_in_b[:, :2 * Q_W]) * rs1

        cos = cos_tab[r0:r0 + RP, :]
        sin = sin_tab[r0:r0 + RP, :]
        ka, kb, v = kv[:, :LANES], kv[:, LANES:2 * LANES], kv[:, 2 * LANES:]
        k_il[0, r0:r0 + RP, :] = ka * cos - kb * sin
        k_il[1, r0:r0 + RP, :] = kb * cos + ka * sin
        v_il[0, r0:r0 + RP, :] = v[:, :LANES]
        v_il[1, r0:r0 + RP, :] = v[:, LANES:]
        for j in range(GROUP):
            qa = ub[:, j * LANES:(j + 1) * LANES]
            qb = ub[:, Q_W // 2 + j * LANES:Q_W // 2 + (j + 1) * LANES]
            q_il[j, r0:r0 + RP, :] = (qa * cos - qb * sin) * scale
            q_il[GROUP + j, r0:r0 + RP, :] = (qb * cos + qa * sin) * scale
        gate2_parts.append(ub[:, Q_W:])
    hcar_ref[...] = h
    h_out[...] = 0.5 * h

    @pl.when(t_idx >= NT - n_kout)
    def _():
        tt = t_idx - (NT - n_kout)
        cps = []
        for l in range(KV_W // LANES):
            cps.append(pltpu.make_async_copy(k_il.at[l], k_out.at[g_idx, tt, :, pl.ds(l * LANES, LANES)],
                                             sem_kv.at[0]))
            cps.append(pltpu.make_async_copy(v_il.at[l], v_out.at[g_idx, tt, :, pl.ds(l * LANES, LANES)],
                                             sem_kv.at[1]))
        for c in cps:
            c.start()
        for c in cps:
            c.wait()

    if has_state:
        cur_off = WINDOW
        n_valid = WINDOW + TT
    else:
        ring = WINDOW // TT + 1
        cur_off = pl.multiple_of((t_idx % ring) * TT, TT)
        n_valid = jnp.minimum(t_idx + 1, ring) * TT

    lane_q = lax.broadcasted_iota(jnp.int32, (TT, 2 * LANES), 1)
    q_slot = (lane_q & (LANES - 1)) // HALF
    first_half = lax.broadcasted_iota(jnp.int32, (TT, LANES), 1) < HEAD_DIM
    invalid = lax.broadcasted_iota(jnp.int32, (TT, KEY_WIN), 1) >= n_valid
    col_row = lax.broadcasted_iota(jnp.int32, (1, KEY_WIN), 1)

    def attend_stream(b, carry):
        rows = pl.ds(b, TT, stride=NB)
        kbuf[b, pl.ds(cur_off, TT), :] = jnp.concatenate(
            [k_il[0, rows, :], k_il[1, rows, :]], axis=1).astype(bf16)
        vbuf[b, pl.ds(cur_off, TT), :] = jnp.concatenate(
            [v_il[0, rows, :], v_il[1, rows, :]], axis=1).astype(bf16)
        qcat = [jnp.concatenate([q_il[j, rows, :], q_il[GROUP + j, rows, :]], axis=1) for j in range(GROUP)]
        qstack = jnp.concatenate(
            [jnp.where(q_slot == h // GROUP, qcat[h % GROUP], 0.0) for h in range(N_HEADS)],
            axis=0).astype(bf16)
        s_all = lax.dot_general(qstack, kbuf[b], (((1,), (1,)), ((), ())),
                                preferred_element_type=f32)
        p_parts, row_sum = [], []
        for h in range(N_HEADS):
            fill = jnp.where(col_row == n_valid, sinks[h] * LOG2E, -jnp.inf)
            sh = jnp.where(invalid, fill, s_all[h * TT:(h + 1) * TT, :])
            ph = jnp.exp2(sh - jnp.max(sh, axis=-1, keepdims=True))
            row_sum.append(jnp.sum(ph, axis=-1, keepdims=True))
            p_parts.append(ph.astype(bf16))
        o_g = [_bdot(jnp.concatenate(p_parts[g * GROUP:(g + 1) * GROUP], axis=0),
                     vbuf[b, :, (g // 2) * LANES:(g // 2 + 1) * LANES]) for g in range(N_KV_HEADS)]
        for j in range(GROUP):
            for half in range(2):
                g0, g1 = 2 * half, 2 * half + 1
                num = jnp.where(first_half, o_g[g0][j * TT:(j + 1) * TT, :], o_g[g1][j * TT:(j + 1) * TT, :])
                den = jnp.where(first_half, row_sum[g0 * GROUP + j], row_sum[g1 * GROUP + j])
                o_il[2 * j + half, rows, :] = num * (1.0 / den)
        return carry

    lax.fori_loop(0, NB, attend_stream, 0, unroll=8)

    @pl.when(step >= 1)
    def _():
        for c in out_copies(t_idx, g_idx):
            c.wait()

    for part in range(n_parts):
        r0 = part * RP
        o2 = jnp.concatenate([o_il[l, r0:r0 + RP, :] for l in range(Q_W // LANES)], axis=1)
        gate2 = gate2_parts[part]
        og = (o2 * (gate2 * _one_plus_tanh(gate2))).astype(bf16)
        yb = _bdot(og, w_out_b[:, :D_MODEL])
        y = x1_parts[part] + yb * rms_scale(yb) * p_model[P_POST_B:P_POST_B + 1, :]
        out_buf[part * FP:(part + 1) * FP] = y.reshape(FP, NB, D_MODEL)
    for c in out_copies(t_idx, g_idx):
        c.start()

    @pl.when(step == n_steps - 1)
    def _():
        for c in out_copies(t_idx, g_idx):
            c.wait()


def _const_spec(shape):
    nd = len(shape)
    return pl.BlockSpec(shape, lambda g, t, _nd=nd: (0,) * _nd, pipeline_mode=pl.Buffered(1))


def _run_trunk(x, state, weights, *, TT, pos0, n_kout, name):
    B, S, _ = x.shape
    NT = S // TT
    NG = B // NB
    R = TT * NB
    assert NT * TT == S and NG * NB == B and n_kout <= NT
    assert state is not None or WINDOW % TT == 0
    has_state = state is not None

    in_specs = [pl.BlockSpec(memory_space=pl.ANY)]
    args = [x]
    if has_state:
        conv0, h0, kh0, vh0 = state
        in_specs += [
            pl.BlockSpec((None, TAIL, D_RNN), lambda g, t: (g, 0, 0)),
            pl.BlockSpec((None, NB, D_RNN), lambda g, t: (g, 0, 0)),
            pl.BlockSpec((NB, WINDOW, KV_W), lambda g, t: (g, 0, 0)),
            pl.BlockSpec((NB, WINDOW, KV_W), lambda g, t: (g, 0, 0)),
        ]
        args += [conv0, h0, kh0, vh0]
    for name_w, w in weights:
        if name_w == "sinks":
            in_specs.append(pl.BlockSpec(memory_space=pltpu.SMEM))
        else:
            in_specs.append(_const_spec(w.shape))
        args.append(w)

    out_shape = (
        jax.ShapeDtypeStruct((B, S, D_MODEL), jnp.float32),
        jax.ShapeDtypeStruct((NG, TAIL, D_RNN), jnp.float32),
        jax.ShapeDtypeStruct((NG, NB, D_RNN), jnp.float32),
        jax.ShapeDtypeStruct((NG, n_kout, R, KV_W), jnp.float32),
        jax.ShapeDtypeStruct((NG, n_kout, R, KV_W), jnp.float32),
    )
    out_specs = (
        pl.BlockSpec(memory_space=pl.ANY),
        pl.BlockSpec((None, TAIL, D_RNN), lambda g, t: (g, 0, 0)),
        pl.BlockSpec((None, NB, D_RNN), lambda g, t: (g, 0, 0)),
        pl.BlockSpec(memory_space=pl.ANY),
        pl.BlockSpec(memory_space=pl.ANY),
    )
    scratch = [
        pltpu.VMEM((2, TT, NB, D_MODEL), jnp.float32),
        pltpu.VMEM((TT, NB, D_MODEL), jnp.float32),
        pltpu.SemaphoreType.DMA((2,)),
        pltpu.SemaphoreType.DMA((1,)),
        pltpu.SemaphoreType.DMA((2,)),
        pltpu.VMEM((TAIL + R, D_RNN), jnp.float32),
        pltpu.VMEM((TAIL, D_RNN), jnp.float32),
        pltpu.VMEM((NB, D_RNN), jnp.float32),
        pltpu.VMEM((NB, KEY_WIN, KV_W), jnp.bfloat16),
        pltpu.VMEM((NB, KEY_WIN, KV_W), jnp.bfloat16),
        pltpu.VMEM((Q_W // LANES, R, LANES), jnp.float32),
        pltpu.VMEM((KV_W // LANES, R, LANES), jnp.float32),
        pltpu.VMEM((KV_W // LANES, R, LANES), jnp.float32),
        pltpu.VMEM((Q_W // LANES, R, LANES), jnp.float32),
        pltpu.VMEM((2, S, LANES), jnp.float32),
        pltpu.VMEM((R, LANES), jnp.float32),
        pltpu.VMEM((R, LANES), jnp.float32),
    ]
    kern = functools.partial(_trunk_kernel, TT=TT, NT=NT, NG=NG, has_state=has_state, pos0=pos0, n_kout=n_kout,
                             n_parts=N_PARTS if TT % (N_PARTS * 2) == 0 else 1)
    return pl.pallas_call(
        kern,
        grid=(NG, NT),
        in_specs=in_specs,
        out_specs=out_specs,
        out_shape=out_shape,
        scratch_shapes=scratch,
        compiler_params=pltpu.CompilerParams(
            dimension_semantics=("arbitrary", "arbitrary"),
            vmem_limit_bytes=58 * 1024 * 1024),
        name=name,
    )(*args)


def _prep_weights(norm_pre_a, w_in_a, conv_w_a, conv_b_a, w_gate_r, b_gate_r, w_gate_i, b_gate_i,
                  lru_lambda, w_out_a, norm_post_a, norm_kv, w_kv, norm_pre_b, w_in_b, attn_sinks,
                  w_out_b, norm_post_b):
    bf16 = jnp.bfloat16
    f32 = jnp.float32
    lane_pad = lambda w: jnp.pad(w, ((0, 0), (0, W_PAD)))

    w_g = (0.5 * jnp.stack([w_gate_r[0], w_gate_i[0]])).astype(bf16)
    dense = jnp.concatenate(
        [jnp.pad(w_g[:, n], ((0, 0), (0, 0), (RG_BLOCK * n, D_RNN - RG_BLOCK * (n + 1))))
         for n in range(N_RG_BLOCKS)], axis=1)
    tiles = []
    for k0, c0, c1 in GATE_WINDOWS:
        t = jnp.pad(dense[:, k0:k0 + GATE_K, c0:c1], ((0, 0), (0, 0), (0, MXU_DIM - (c1 - c0))))
        tiles.append(jnp.concatenate([t[0], t[1]], axis=1))
    w_ri = jnp.stack(tiles)

    wk = w_kv[:, :KV_W].astype(bf16).reshape(D_MODEL, N_KV_HEADS, 2, HALF)
    w_kv_p = jnp.concatenate([wk.transpose(0, 2, 1, 3).reshape(D_MODEL, KV_W), w_kv[:, KV_W:].astype(bf16)], axis=1)
    wb = w_in_b[0].astype(bf16)
    wq = wb[:, :Q_W].reshape(D_MODEL, N_KV_HEADS, GROUP, 2, HALF).transpose(0, 3, 2, 1, 4).reshape(D_MODEL, Q_W)
    wg = (0.5 * wb[:, Q_W:]).reshape(D_MODEL, N_KV_HEADS, GROUP, HEAD_DIM).transpose(0, 2, 1, 3).reshape(D_MODEL, Q_W)
    w_in_b_p = jnp.concatenate([wq, wg], axis=1)
    w_out_b_p = w_out_b[0].astype(bf16).reshape(N_KV_HEADS, GROUP, HEAD_DIM, D_MODEL).transpose(1, 0, 2, 3).reshape(
        Q_W, D_MODEL)

    inv = ROPE_THETA ** (-jnp.arange(HALF, dtype=f32) / HALF)
    inv_row = jnp.pad(jnp.tile(inv, LANES // HALF), (0, D_MODEL - LANES))
    zeros_m = jnp.zeros((D_MODEL,), f32)
    p_model = jnp.stack([norm_pre_a[0], norm_post_a[0], norm_kv, norm_pre_b[0], norm_post_b[0], inv_row,
                         zeros_m, zeros_m]).astype(f32)
    p_rnn = jnp.concatenate([conv_w_a[0], conv_b_a, 0.5 * b_gate_r, 0.5 * b_gate_i, lru_lambda], axis=0).astype(f32)
    in_a_scale = jnp.concatenate([jnp.ones((D_RNN,), f32), jnp.full((D_RNN,), 0.5, f32)])

    return [
        ("p_model", p_model),
        ("p_rnn", p_rnn),
        ("w_in_a", (w_in_a[0] * in_a_scale).astype(bf16)),
        ("w_ri", w_ri),
        ("w_out_a", lane_pad((0.5 * w_out_a[0]).astype(bf16))),
        ("w_kv", w_kv_p),
        ("w_in_b", lane_pad(w_in_b_p)),
        ("sinks", attn_sinks[0].astype(f32)),
        ("w_out_b", lane_pad(w_out_b_p)),
    ]


def _k_to_cache_layout(k):
    B, R, _ = k.shape
    ka = k[..., :LANES].reshape(B, R, N_KV_HEADS, HALF)
    kb = k[..., LANES:].reshape(B, R, N_KV_HEADS, HALF)
    return jnp.concatenate([ka, kb], axis=-1)


def _k_from_cache_layout(k):
    B, R = k.shape[:2]
    return jnp.concatenate([k[..., :HALF].reshape(B, R, LANES), k[..., HALF:].reshape(B, R, LANES)], axis=-1)


def _to_streams(a, frames):
    NG, n, _, C = a.shape
    return a.reshape(NG, n, frames, NB, C).transpose(0, 3, 1, 2, 4).reshape(NG * NB, n * frames, C)


def kernel(x_prompt, x_sample, state_conv, state_rnn, cache_k, cache_v, norm_pre_a, w_in_a, conv_w_a, conv_b_a, w_gate_r, b_gate_r, w_gate_i, b_gate_i, lru_lambda, w_out_a, norm_post_a, norm_kv, w_kv, norm_pre_b, w_in_b, attn_sinks, w_out_b, norm_post_b):
    weights = _prep_weights(norm_pre_a, w_in_a, conv_w_a, conv_b_a, w_gate_r, b_gate_r, w_gate_i, b_gate_i,
                            lru_lambda, w_out_a, norm_post_a, norm_kv, w_kv, norm_pre_b, w_in_b,
                            attn_sinks, w_out_b, norm_post_b)
    tail = CONV_W - 1

    def unpack(y, t_out, h_out, k_o, v_o, frames):
        NG = t_out.shape[0]
        conv = t_out.reshape(NG, tail, NB, D_RNN).transpose(0, 2, 1, 3).reshape(NG * NB, tail, D_RNN)
        k = _k_to_cache_layout(_to_streams(k_o, frames))
        v = _to_streams(v_o, frames)
        return (y, conv[None], h_out.reshape(NG * NB, D_RNN)[None], k,
                v.reshape(v.shape[0], v.shape[1], N_KV_HEADS, HEAD_DIM))

    prompt = _run_trunk(x_prompt, None, weights, TT=CHUNK, pos0=0, n_kout=WINDOW // CHUNK, name="trunk_prompt")
    yp, pc, pr, pk, pv = unpack(*prompt, CHUNK)

    Bs, Ts, _ = x_sample.shape
    NGs = Bs // NB
    conv0 = state_conv[0].reshape(NGs, NB, tail, D_RNN).transpose(0, 2, 1, 3).reshape(NGs, TAIL, D_RNN)
    h0 = state_rnn[0].reshape(NGs, NB, D_RNN)
    kh0 = _k_from_cache_layout(cache_k)
    vh0 = cache_v.reshape(Bs, WINDOW, KV_W)
    sample = _run_trunk(x_sample, (conv0, h0, kh0, vh0), weights, TT=Ts, pos0=PAST_LEN, n_kout=1,
                        name="trunk_sample")
    ys, sc, sr, sk, sv = unpack(*sample, Ts)

    return (yp, ys, pc, pr, pk, pv, sc, sr, sk, sv)
```

```python
import functools

import jax
import jax.numpy as jnp
from jax import lax
from jax.experimental import pallas as pl
from jax.experimental.pallas import tpu as pltpu

D_MODEL = 1024
D_RNN = 1408
N_RG_BLOCKS = 16
RG_BLOCK = D_RNN // N_RG_BLOCKS
CONV_W = 4
LRU_C = 8.0
N_HEADS = 16
N_KV_HEADS = 4
HEAD_DIM = 64
HALF = HEAD_DIM // 2
GROUP = N_HEADS // N_KV_HEADS
WINDOW = 128
CHUNK = 64
PAST_LEN = 1024
ROPE_THETA = 10000.0
EPS = 1e-6
LOG2E = 1.4426950408889634

KV_W = N_KV_HEADS * HEAD_DIM
Q_W = N_HEADS * HEAD_DIM

LANES = 128
SUBLANES = 8
MXU_DIM = 256

NB = SUBLANES
TAIL = (CONV_W - 1) * NB

GATE_K = 512
N_GATE_TILES = -(-D_RNN // MXU_DIM)

KEY_WIN = 256
VMEM_LIMIT_BYTES = 58 * 1024 * 1024
W_PAD = LANES

P_PRE_A, P_POST_A, P_KV, P_PRE_B, P_POST_B, P_INV_FREQ = range(6)
P_CONV_B, P_B_R, P_B_I, P_LAMBDA = range(CONV_W, CONV_W + 4)


def _gate_windows():
    wins = []
    for j in range(N_GATE_TILES):
        c0 = j * MXU_DIM
        c1 = min(c0 + MXU_DIM, D_RNN)
        lo = (c0 // RG_BLOCK) * RG_BLOCK
        hi = ((c1 - 1) // RG_BLOCK + 1) * RG_BLOCK
        k0 = min((lo // LANES) * LANES, D_RNN - GATE_K)
        assert k0 <= lo and hi <= k0 + GATE_K
        wins.append((k0, c0, c1))
    return wins


GATE_WINDOWS = _gate_windows()


def _bdot(a, b):
    return jnp.dot(a, b, preferred_element_type=jnp.float32)


def _one_plus_tanh(half_x):
    return jnp.tanh(half_x) + 1.0


def _trunk_kernel(*refs, TT, NT, NG, has_state, pos0, n_kout):
    it = iter(refs)
    x_hbm = next(it)
    if has_state:
        conv0_ref, h0_ref, kh0_ref, vh0_ref = next(it), next(it), next(it), next(it)
    p_model, p_rnn, w_in_a, w_ri, w_out_a, w_kv, w_in_b, sinks, w_out_b = [next(it) for _ in range(9)]
    y_hbm, tail_out, h_out, k_out, v_out = [next(it) for _ in range(5)]
    (in_buf, out_buf, sem_in, sem_out, sem_kv, ext_ref, tail_ref, hcar_ref, kbuf, vbuf,
     q_il, k_il, v_il, o_il, rope_small, cos_tab, sin_tab) = [next(it) for _ in range(17)]

    g_idx = pl.program_id(0)
    t_idx = pl.program_id(1)
    step = g_idx * NT + t_idx
    n_steps = NT * NG
    slot = step & 1
    R = TT * NB
    f32 = jnp.float32
    bf16 = jnp.bfloat16

    def in_copies(t, g, s):
        return [pltpu.make_async_copy(x_hbm.at[g * NB + b, pl.ds(t * TT, TT), :],
                                      in_buf.at[s, :, b, :], sem_in.at[s]) for b in range(NB)]

    def out_copies(t, g):
        return [pltpu.make_async_copy(out_buf.at[:, b, :],
                                      y_hbm.at[g * NB + b, pl.ds(t * TT, TT), :], sem_out.at[0]) for b in range(NB)]

    @pl.when(step == 0)
    def _():
        for c in in_copies(t_idx, g_idx, slot):
            c.start()

    @pl.when(step + 1 < n_steps)
    def _():
        nxt = step + 1
        for c in in_copies(nxt % NT, nxt // NT, 1 - slot):
            c.start()

    for c in in_copies(t_idx, g_idx, slot):
        c.wait()

    if has_state:
        @pl.when(t_idx == 0)
        def _():
            ext_ref[0:TAIL, :] = conv0_ref[...]
            hcar_ref[...] = 2.0 * h0_ref[...]
            zpad = jnp.zeros((NB, KEY_WIN - WINDOW, KV_W), bf16)
            kbuf[:, 0:WINDOW, :] = kh0_ref[...].astype(bf16)
            vbuf[:, 0:WINDOW, :] = vh0_ref[...].astype(bf16)
            kbuf[:, WINDOW:, :] = zpad
            vbuf[:, WINDOW:, :] = zpad
    else:
        @pl.when(t_idx == 0)
        def _():
            ext_ref[0:TAIL, :] = jnp.zeros((TAIL, D_RNN), f32)
            hcar_ref[...] = jnp.zeros((NB, D_RNN), f32)
            kbuf[...] = jnp.zeros(kbuf.shape, bf16)
            vbuf[...] = jnp.zeros(vbuf.shape, bf16)

    @pl.when(t_idx > 0)
    def _():
        ext_ref[0:TAIL, :] = tail_ref[...]

    f0 = pl.multiple_of(t_idx * TT, TT)

    @pl.when(g_idx == 0)
    def _():
        pos = (lax.broadcasted_iota(jnp.int32, (TT, LANES), 0) + (pos0 + t_idx * TT)).astype(f32)
        ang = pos * p_model[P_INV_FREQ:P_INV_FREQ + 1, :LANES]
        rope_small[0, pl.ds(f0, TT), :] = jnp.cos(ang)
        rope_small[1, pl.ds(f0, TT), :] = jnp.sin(ang)

    for t in range(TT):
        cos_tab[t * NB:(t + 1) * NB, :] = jnp.broadcast_to(rope_small[0, pl.ds(f0 + t, 1), :], (NB, LANES))
        sin_tab[t * NB:(t + 1) * NB, :] = jnp.broadcast_to(rope_small[1, pl.ds(f0 + t, 1), :], (NB, LANES))

    def rms_scale(v):
        return lax.rsqrt(jnp.sum(v * v, axis=-1, keepdims=True) * (1.0 / D_MODEL) + EPS)

    x = in_buf[slot].reshape(R, D_MODEL)
    xg = (x * p_model[P_PRE_A:P_PRE_A + 1, :]).astype(bf16)
    u = _bdot(xg, w_in_a[...]) * rms_scale(x)
    branch = u[:, :D_RNN]
    gate = u[:, D_RNN:]

    ext_ref[TAIL:TAIL + R, :] = branch
    conv = p_rnn[P_CONV_B:P_CONV_B + 1, :] + branch * p_rnn[CONV_W - 1:CONV_W, :]
    for tap in range(CONV_W - 1):
        conv = conv + ext_ref[tap * NB:tap * NB + R, :] * p_rnn[tap:tap + 1, :]
    new_tail = branch[R - TAIL:, :]
    tail_ref[...] = new_tail
    tail_out[...] = new_tail

    conv_bf = conv.astype(bf16)
    r_parts, i_parts = [], []
    for j, (k0, c0, c1) in enumerate(GATE_WINDOWS):
        rg = _bdot(conv_bf[:, k0:k0 + GATE_K], w_ri[j])
        r_parts.append(rg[:, :c1 - c0])
        i_parts.append(rg[:, MXU_DIM:MXU_DIM + c1 - c0])
    tr = jnp.tanh(jnp.concatenate(r_parts, axis=1) + p_rnn[P_B_R:P_B_R + 1, :])
    i2 = _one_plus_tanh(jnp.concatenate(i_parts, axis=1) + p_rnn[P_B_I:P_B_I + 1, :])

    z = -p_rnn[P_LAMBDA:P_LAMBDA + 1, :]
    softplus = jnp.maximum(z, 0.0) + jnp.log1p(jnp.exp(-jnp.abs(z)))
    half_log2_a = (-0.5 * LRU_C * LOG2E) * softplus
    a = jnp.exp2(half_log2_a * tr + half_log2_a)
    one_m_a2 = 1.0 - a * a
    root = jnp.where(one_m_a2 > 0.0, one_m_a2 * lax.rsqrt(one_m_a2), 0.0)
    b2 = root * (i2 * conv)

    h = hcar_ref[...]
    hs = []
    for t in range(TT):
        h = a[t * NB:(t + 1) * NB, :] * h + b2[t * NB:(t + 1) * NB, :]
        hs.append(h)
    hcar_ref[...] = h
    h_out[...] = 0.5 * h
    h_all = jnp.concatenate(hs, axis=0)

    ya = _bdot((h_all * (gate * _one_plus_tanh(gate))).astype(bf16), w_out_a[:, :D_MODEL])
    x1 = x + ya * rms_scale(ya) * p_model[P_POST_A:P_POST_A + 1, :]

    rs1 = rms_scale(x1)
    kv = _bdot((x1 * p_model[P_KV:P_KV + 1, :]).astype(bf16), w_kv[...]) * rs1
    ub = _bdot((x1 * p_model[P_PRE_B:P_PRE_B + 1, :]).astype(bf16), w_in_b[:, :2 * Q_W]) * rs1

    cos = cos_tab[...]
    sin = sin_tab[...]
    ka, kb, v = kv[:, :LANES], kv[:, LANES:2 * LANES], kv[:, 2 * LANES:]
    k_il[0] = ka * cos - kb * sin
    k_il[1] = kb * cos + ka * sin
    v_il[0] = v[:, :LANES]
    v_il[1] = v[:, LANES:]
    scale = HEAD_DIM ** -0.5 * LOG2E
    for j in range(GROUP):
        qa = ub[:, j * LANES:(j + 1) * LANES]
        qb = ub[:, Q_W // 2 + j * LANES:Q_W // 2 + (j + 1) * LANES]
        q_il[j] = (qa * cos - qb * sin) * scale
        q_il[GROUP + j] = (qb * cos + qa * sin) * scale
    gate2 = ub[:, Q_W:]

    @pl.when(t_idx >= NT - n_kout)
    def _():
        tt = t_idx - (NT - n_kout)
        cps = []
        for l in range(KV_W // LANES):
            cps.append(pltpu.make_async_copy(k_il.at[l], k_out.at[g_idx, tt, :, pl.ds(l * LANES, LANES)],
                                             sem_kv.at[0]))
            cps.append(pltpu.make_async_copy(v_il.at[l], v_out.at[g_idx, tt, :, pl.ds(l * LANES, LANES)],
                                             sem_kv.at[1]))
        for c in cps:
            c.start()
        for c in cps:
            c.wait()

    if has_state:
        cur_off = WINDOW
        n_valid = WINDOW + TT
    else:
        ring = WINDOW // TT + 1
        cur_off = pl.multiple_of((t_idx % ring) * TT, TT)
        n_valid = jnp.minimum(t_idx + 1, ring) * TT

    lane_q = lax.broadcasted_iota(jnp.int32, (TT, 2 * LANES), 1)
    q_slot = (lane_q & (LANES - 1)) // HALF
    first_half = lax.broadcasted_iota(jnp.int32, (TT, LANES), 1) < HEAD_DIM
    invalid = lax.broadcasted_iota(jnp.int32, (TT, KEY_WIN), 1) >= n_valid
    col_row = lax.broadcasted_iota(jnp.int32, (1, KEY_WIN), 1)

    def attend_stream(b, carry):
        rows = pl.ds(b, TT, stride=NB)
        kbuf[b, pl.ds(cur_off, TT), :] = jnp.concatenate(
            [k_il[0, rows, :], k_il[1, rows, :]], axis=1).astype(bf16)
        vbuf[b, pl.ds(cur_off, TT), :] = jnp.concatenate(
            [v_il[0, rows, :], v_il[1, rows, :]], axis=1).astype(bf16)
        qcat = [jnp.concatenate([q_il[j, rows, :], q_il[GROUP + j, rows, :]], axis=1) for j in range(GROUP)]
        qstack = jnp.concatenate(
            [jnp.where(q_slot == h // GROUP, qcat[h % GROUP], 0.0) for h in range(N_HEADS)],
            axis=0).astype(bf16)
        s_all = lax.dot_general(qstack, kbuf[b], (((1,), (1,)), ((), ())),
                                preferred_element_type=f32)
        p_parts, row_sum = [], []
        for h in range(N_HEADS):
            fill = jnp.where(col_row == n_valid, sinks[h] * LOG2E, -jnp.inf)
            sh = jnp.where(invalid, fill, s_all[h * TT:(h + 1) * TT, :])
            ph = jnp.exp2(sh - jnp.max(sh, axis=-1, keepdims=True))
            row_sum.append(jnp.sum(ph, axis=-1, keepdims=True))
            p_parts.append(ph.astype(bf16))
        o_g = [_bdot(jnp.concatenate(p_parts[g * GROUP:(g + 1) * GROUP], axis=0),
                     vbuf[b, :, (g // 2) * LANES:(g // 2 + 1) * LANES]) for g in range(N_KV_HEADS)]
        for j in range(GROUP):
            for half in range(2):
                g0, g1 = 2 * half, 2 * half + 1
                num = jnp.where(first_half, o_g[g0][j * TT:(j + 1) * TT, :], o_g[g1][j * TT:(j + 1) * TT, :])
                den = jnp.where(first_half, row_sum[g0 * GROUP + j], row_sum[g1 * GROUP + j])
                o_il[2 * j + half, rows, :] = num * (1.0 / den)
        return carry

    lax.fori_loop(0, NB, attend_stream, 0, unroll=8)

    @pl.when(step >= 1)
    def _():
        for c in out_copies(t_idx, g_idx):
            c.wait()

    o2 = jnp.concatenate([o_il[l] for l in range(Q_W // LANES)], axis=1)
    og = (o2 * (gate2 * _one_plus_tanh(gate2))).astype(bf16)
    yb = _bdot(og, w_out_b[:, :D_MODEL])
    y = x1 + yb * rms_scale(yb) * p_model[P_POST_B:P_POST_B + 1, :]
    out_buf[...] = y.reshape(TT, NB, D_MODEL)
    for c in out_copies(t_idx, g_idx):
        c.start()

    @pl.when(step == n_steps - 1)
    def _():
        for c in out_copies(t_idx, g_idx):
            c.wait()


def _const_spec(shape):
    nd = len(shape)
    return pl.BlockSpec(shape, lambda g, t, _nd=nd: (0,) * _nd, pipeline_mode=pl.Buffered(1))


def _run_trunk(x, state, weights, *, TT, pos0, n_kout, name):
    B, S, _ = x.shape
    NT = S // TT
    NG = B // NB
    R = TT * NB
    assert NT * TT == S and NG * NB == B and n_kout <= NT
    assert state is not None or WINDOW % TT == 0
    has_state = state is not None

    in_specs = [pl.BlockSpec(memory_space=pl.ANY)]
    args = [x]
    if has_state:
        conv0, h0, kh0, vh0 = state
        in_specs += [
            pl.BlockSpec((None, TAIL, D_RNN), lambda g, t: (g, 0, 0)),
            pl.BlockSpec((None, NB, D_RNN), lambda g, t: (g, 0, 0)),
            pl.BlockSpec((NB, WINDOW, KV_W), lambda g, t: (g, 0, 0)),
            pl.BlockSpec((NB, WINDOW, KV_W), lambda g, t: (g, 0, 0)),
        ]
        args += [conv0, h0, kh0, vh0]
    for name_w, w in weights:
        if name_w == "sinks":
            in_specs.append(pl.BlockSpec(memory_space=pltpu.SMEM))
        else:
            in_specs.append(_const_spec(w.shape))
        args.append(w)

    out_shape = (
        jax.ShapeDtypeStruct((B, S, D_MODEL), jnp.float32),
        jax.ShapeDtypeStruct((NG, TAIL, D_RNN), jnp.float32),
        jax.ShapeDtypeStruct((NG, NB, D_RNN), jnp.float32),
        jax.ShapeDtypeStruct((NG, n_kout, R, KV_W), jnp.float32),
        jax.ShapeDtypeStruct((NG, n_kout, R, KV_W), jnp.float32),
    )
    out_specs = (
        pl.BlockSpec(memory_space=pl.ANY),
        pl.BlockSpec((None, TAIL, D_RNN), lambda g, t: (g, 0, 0)),
        pl.BlockSpec((None, NB, D_RNN), lambda g, t: (g, 0, 0)),
        pl.BlockSpec(memory_space=pl.ANY),
        pl.BlockSpec(memory_space=pl.ANY),
    )
    scratch = [
        pltpu.VMEM((2, TT, NB, D_MODEL), jnp.float32),
        pltpu.VMEM((TT, NB, D_MODEL), jnp.float32),
        pltpu.SemaphoreType.DMA((2,)),
        pltpu.SemaphoreType.DMA((1,)),
        pltpu.SemaphoreType.DMA((2,)),
        pltpu.VMEM((TAIL + R, D_RNN), jnp.float32),
        pltpu.VMEM((TAIL, D_RNN), jnp.float32),
        pltpu.VMEM((NB, D_RNN), jnp.float32),
        pltpu.VMEM((NB, KEY_WIN, KV_W), jnp.bfloat16),
        pltpu.VMEM((NB, KEY_WIN, KV_W), jnp.bfloat16),
        pltpu.VMEM((Q_W // LANES, R, LANES), jnp.float32),
        pltpu.VMEM((KV_W // LANES, R, LANES), jnp.float32),
        pltpu.VMEM((KV_W // LANES, R, LANES), jnp.float32),
        pltpu.VMEM((Q_W // LANES, R, LANES), jnp.float32),
        pltpu.VMEM((2, S, LANES), jnp.float32),
        pltpu.VMEM((R, LANES), jnp.float32),
        pltpu.VMEM((R, LANES), jnp.float32),
    ]
    kern = functools.partial(_trunk_kernel, TT=TT, NT=NT, NG=NG, has_state=has_state, pos0=pos0, n_kout=n_kout)
    return pl.pallas_call(
        kern,
        grid=(NG, NT),
        in_specs=in_specs,
        out_specs=out_specs,
        out_shape=out_shape,
        scratch_shapes=scratch,
        compiler_params=pltpu.CompilerParams(
            dimension_semantics=("arbitrary", "arbitrary"),
            vmem_limit_bytes=VMEM_LIMIT_BYTES),
        name=name,
    )(*args)


def _prep_weights(norm_pre_a, w_in_a, conv_w_a, conv_b_a, w_gate_r, b_gate_r, w_gate_i, b_gate_i,
                  lru_lambda, w_out_a, norm_post_a, norm_kv, w_kv, norm_pre_b, w_in_b, attn_sinks,
                  w_out_b, norm_post_b):
    bf16 = jnp.bfloat16
    f32 = jnp.float32
    lane_pad = lambda w: jnp.pad(w, ((0, 0), (0, W_PAD)))

    w_g = (0.5 * jnp.stack([w_gate_r[0], w_gate_i[0]])).astype(bf16)
    dense = jnp.concatenate(
        [jnp.pad(w_g[:, n], ((0, 0), (0, 0), (RG_BLOCK * n, D_RNN - RG_BLOCK * (n + 1))))
         for n in range(N_RG_BLOCKS)], axis=1)
    tiles = []
    for k0, c0, c1 in GATE_WINDOWS:
        t = jnp.pad(dense[:, k0:k0 + GATE_K, c0:c1], ((0, 0), (0, 0), (0, MXU_DIM - (c1 - c0))))
        tiles.append(jnp.concatenate([t[0], t[1]], axis=1))
    w_ri = jnp.stack(tiles)

    wk = w_kv[:, :KV_W].astype(bf16).reshape(D_MODEL, N_KV_HEADS, 2, HALF)
    w_kv_p = jnp.concatenate([wk.transpose(0, 2, 1, 3).reshape(D_MODEL, KV_W), w_kv[:, KV_W:].astype(bf16)], axis=1)
    wb = w_in_b[0].astype(bf16)
    wq = wb[:, :Q_W].reshape(D_MODEL, N_KV_HEADS, GROUP, 2, HALF).transpose(0, 3, 2, 1, 4).reshape(D_MODEL, Q_W)
    wg = (0.5 * wb[:, Q_W:]).reshape(D_MODEL, N_KV_HEADS, GROUP, HEAD_DIM).transpose(0, 2, 1, 3).reshape(D_MODEL, Q_W)
    w_in_b_p = jnp.concatenate([wq, wg], axis=1)
    w_out_b_p = w_out_b[0].astype(bf16).reshape(N_KV_HEADS, GROUP, HEAD_DIM, D_MODEL).transpose(1, 0, 2, 3).reshape(
        Q_W, D_MODEL)

    inv = ROPE_THETA ** (-jnp.arange(HALF, dtype=f32) / HALF)
    inv_row = jnp.pad(jnp.tile(inv, LANES // HALF), (0, D_MODEL - LANES))
    zeros_m = jnp.zeros((D_MODEL,), f32)
    p_model = jnp.stack([norm_pre_a[0], norm_post_a[0], norm_kv, norm_pre_b[0], norm_post_b[0], inv_row,
                         zeros_m, zeros_m]).astype(f32)
    p_rnn = jnp.concatenate([conv_w_a[0], conv_b_a, 0.5 * b_gate_r, 0.5 * b_gate_i, lru_lambda], axis=0).astype(f32)
    in_a_scale = jnp.concatenate([jnp.ones((D_RNN,), f32), jnp.full((D_RNN,), 0.5, f32)])

    return [
        ("p_model", p_model),
        ("p_rnn", p_rnn),
        ("w_in_a", (w_in_a[0] * in_a_scale).astype(bf16)),
        ("w_ri", w_ri),
        ("w_out_a", lane_pad((0.5 * w_out_a[0]).astype(bf16))),
        ("w_kv", w_kv_p),
        ("w_in_b", lane_pad(w_in_b_p)),
        ("sinks", attn_sinks[0].astype(f32)),
        ("w_out_b", lane_pad(w_out_b_p)),
    ]


def _k_to_cache_layout(k):
    B, R, _ = k.shape
    ka = k[..., :LANES].reshape(B, R, N_KV_HEADS, HALF)
    kb = k[..., LANES:].reshape(B, R, N_KV_HEADS, HALF)
    return jnp.concatenate([ka, kb], axis=-1)


def _k_from_cache_layout(k):
    B, R = k.shape[:2]
    return jnp.concatenate([k[..., :HALF].reshape(B, R, LANES), k[..., HALF:].reshape(B, R, LANES)], axis=-1)


def _to_streams(a, frames):
    NG, n, _, C = a.shape
    return a.reshape(NG, n, frames, NB, C).transpose(0, 3, 1, 2, 4).reshape(NG * NB, n * frames, C)


def kernel(x_prompt, x_sample, state_conv, state_rnn, cache_k, cache_v, norm_pre_a, w_in_a, conv_w_a, conv_b_a, w_gate_r, b_gate_r, w_gate_i, b_gate_i, lru_lambda, w_out_a, norm_post_a, norm_kv, w_kv, norm_pre_b, w_in_b, attn_sinks, w_out_b, norm_post_b):
    weights = _prep_weights(norm_pre_a, w_in_a, conv_w_a, conv_b_a, w_gate_r, b_gate_r, w_gate_i, b_gate_i,
                            lru_lambda, w_out_a, norm_post_a, norm_kv, w_kv, norm_pre_b, w_in_b,
                            attn_sinks, w_out_b, norm_post_b)
    tail = CONV_W - 1

    def unpack(y, t_out, h_out, k_o, v_o, frames):
        NG = t_out.shape[0]
        conv = t_out.reshape(NG, tail, NB, D_RNN).transpose(0, 2, 1, 3).reshape(NG * NB, tail, D_RNN)
        k = _k_to_cache_layout(_to_streams(k_o, frames))
        v = _to_streams(v_o, frames)
        return (y, conv[None], h_out.reshape(NG * NB, D_RNN)[None], k,
                v.reshape(v.shape[0], v.shape[1], N_KV_HEADS, HEAD_DIM))

    prompt = _run_trunk(x_prompt, None, weights, TT=CHUNK, pos0=0, n_kout=WINDOW // CHUNK, name="trunk_prompt")
    yp, pc, pr, pk, pv = unpack(*prompt, CHUNK)

    Bs, Ts, _ = x_sample.shape
    NGs = Bs // NB
    conv0 = state_conv[0].reshape(NGs, NB, tail, D_RNN).transpose(0, 2, 1, 3).reshape(NGs, TAIL, D_RNN)
    h0 = state_rnn[0].reshape(NGs, NB, D_RNN)
    kh0 = _k_from_cache_layout(cache_k)
    vh0 = cache_v.reshape(Bs, WINDOW, KV_W)
    sample = _run_trunk(x_sample, (conv0, h0, kh0, vh0), weights, TT=Ts, pos0=PAST_LEN, n_kout=1,
                        name="trunk_sample")
    ys, sc, sr, sk, sv = unpack(*sample, Ts)

    return (yp, ys, pc, pr, pk, pv, sc, sr, sk, sv)
```
